```python
import jax, jax.numpy as jnp
from jax import lax
import numpy as np

D_MODEL = 4096
BATCH = 1
SEQ = 8192
DEPTH = 1
DEC_BATCH = 16
DEC_SEQ = 64
PAST_LEN = 1024

CHUNK = 64
H_R = 8
DK_R = 256
DV_R = 256
W_R = H_R * DK_R
H_M = 8
DK_M = 256
DV_M = 256
W_M = H_M * DK_M
CONV_W = 4
PLE_DIM = 256
N_EXPERTS = 256
TOP_K = 8
N_GROUPS = 8
TOPK_GROUPS = 4
D_EXPERT = 1024
D_SHARED = 1024
ROUTED_SCALE = 2.5
MOE_BLOCK = 128
ROPE_BASE = 10000.0
LN_EPS = 1e-5
ALPHA = (2 * DEPTH) ** 0.25
BETA = (8 * DEPTH) ** -0.25

OFF_RQ = 0
OFF_RK = OFF_RQ + W_R
OFF_RV = OFF_RK + W_R
OFF_RG = OFF_RV + W_R
OFF_MQ = OFF_RG + W_R
OFF_MK = OFF_MQ + W_M
OFF_MV = OFF_MK + W_M
OFF_MO = OFF_MV + W_M
OFF_MI = OFF_MO + W_M
OFF_MF = OFF_MI + H_M
OFF_GA = OFF_MF + H_M
OFF_GB = OFF_GA + D_MODEL
N_IN = OFF_GB + D_MODEL

kernel_name = 'hybrid_retention_mlstm_moe_stream'


def _layer_norm(x, g, b):
    xf = x.astype(jnp.float32)
    mu = xf.mean(-1, keepdims=True)
    var = jnp.square(xf - mu).mean(-1, keepdims=True)
    y = (xf - mu) * lax.rsqrt(var + LN_EPS) * g.astype(jnp.float32) + b.astype(jnp.float32)
    return y.astype(x.dtype)


def _head_norm(y, g):
    mu = y.mean(-1, keepdims=True)
    var = jnp.square(y - mu).mean(-1, keepdims=True)
    yn = (y - mu) * lax.rsqrt(var + LN_EPS)
    return yn.reshape(y.shape[0], y.shape[1], -1) * g.astype(jnp.float32)


def _rotary(t, pos):
    half = t.shape[-1] // 2
    inv_freq = ROPE_BASE ** (-jnp.arange(half, dtype=jnp.float32) / half)
    ang = pos.astype(jnp.float32)[:, None] * inv_freq[None, :]
    cos = jnp.cos(ang)[None, :, None, :]
    sin = jnp.sin(ang)[None, :, None, :]
    t1, t2 = t[..., :half], t[..., half:]
    return jnp.concatenate([t1 * cos - t2 * sin, t1 * sin + t2 * cos], axis=-1)


def _causal_conv(u_ext, w, b):
    L = u_ext.shape[1] - (CONV_W - 1)
    acc = b
    for j in range(CONV_W):
        acc = acc + w[j] * u_ext[:, j:j + L]
    return acc


def _retention_chunk(q, k, v, s):
    L = q.shape[1]
    log_g = jnp.log1p(-jnp.exp2(-5.0 - jnp.arange(H_R, dtype=jnp.float32)))
    n = jnp.arange(L, dtype=jnp.float32)
    diff = n[:, None] - n[None, :]
    decay = jnp.exp(jnp.where((diff >= 0)[None], diff[None] * log_g[:, None, None], -jnp.inf))
    scores = jnp.einsum('blhd,bmhd->bhlm', q, k) * decay[None]
    inner = jnp.einsum('bhlm,bmhe->blhe', scores, v)
    q_decay = jnp.exp((n[:, None] + 1.0) * log_g[None, :])
    cross = jnp.einsum('blhd,bhde->blhe', q, s) * q_decay[None, :, :, None]
    k_decay = jnp.exp((L - 1.0 - n)[:, None] * log_g[None, :])
    s_new = jnp.exp(L * log_g)[None, :, None, None] * s + jnp.einsum('blhd,blhe,lh->bhde', k, v, k_decay)
    return inner + cross, s_new


def _mlstm_chunk(q, k, v, i_pre, log_f, c, nrm, m):
    L = q.shape[1]
    b = jnp.cumsum(log_f, axis=1)
    a = b + m[:, None, :]
    causal = jnp.tril(jnp.ones((L, L), dtype=bool))
    dmat = b[:, :, None, :] - b[:, None, :, :] + i_pre[:, None, :, :]
    dmat = jnp.where(causal[None, :, :, None], dmat, -jnp.inf)
    m_t = jnp.maximum(a, dmat.max(axis=2))
    w_inter = jnp.exp(a - m_t)
    w_intra = jnp.exp(dmat - m_t[:, :, None, :])
    qk = jnp.einsum('bthd,bshd->btsh', q, k) * w_intra
    num = jnp.einsum('btsh,bshe->bthe', qk, v) + w_inter[..., None] * jnp.einsum('bthd,bhde->bthe', q, c)
    den = qk.sum(axis=2) + w_inter * jnp.einsum('bthd,bhd->bth', q, nrm)
    h = num / jnp.maximum(jnp.abs(den), jnp.exp(-m_t))[..., None]
    wl_inter = w_inter[:, -1]
    wl_intra = w_intra[:, -1]
    c_new = wl_inter[:, :, None, None] * c + jnp.einsum('bsh,bshd,bshe->bhde', wl_intra, k, v)
    n_new = wl_inter[:, :, None] * nrm + jnp.einsum('bsh,bshd->bhd', wl_intra, k)
    return h, c_new, n_new, m_t[:, -1]


def _chunk_step(carry, xs):
    s_ret, c_m, n_m, m_m = carry
    rq, rk, rv, mq, mk, mv, mi, mf = xs
    yr, s_ret = _retention_chunk(rq, rk, rv, s_ret)
    ym, c_m, n_m, m_m = _mlstm_chunk(mq, mk, mv, mi, mf, c_m, n_m, m_m)
    return (s_ret, c_m, n_m, m_m), (yr, ym)


def _to_chunks(a, cl):
    B, L = a.shape[0], a.shape[1]
    return jnp.swapaxes(a.reshape(B, L // cl, cl, *a.shape[2:]), 0, 1)


def _from_chunks(a):
    a = jnp.swapaxes(a, 0, 1)
    return a.reshape(a.shape[0], a.shape[1] * a.shape[2], *a.shape[3:])


def _moe(h, w_router, b_router, w_e_gate, w_e_up, w_e_down, w_s_gate, w_s_up, w_s_down, layer):
    T = h.shape[0]
    f32 = jnp.float32
    scores = jax.nn.sigmoid((h @ w_router).astype(f32))
    sel = scores + b_router.astype(f32)
    grp_score = lax.top_k(sel.reshape(T, N_GROUPS, -1), 2)[0].sum(-1)
    _, grp_idx = lax.top_k(grp_score, TOPK_GROUPS)
    grp_mask = jax.nn.one_hot(grp_idx, N_GROUPS, dtype=f32).sum(1) > 0
    sel = jnp.where(jnp.repeat(grp_mask, N_EXPERTS // N_GROUPS, axis=1), sel, -jnp.inf)
    _, top_idx = lax.top_k(sel, TOP_K)
    top_s = jnp.take_along_axis(scores, top_idx, axis=1)
    gates = top_s / top_s.sum(-1, keepdims=True) * ROUTED_SCALE
    A = T * TOP_K
    flat_e = top_idx.reshape(A)
    flat_t = jnp.arange(A, dtype=jnp.int32) // TOP_K
    flat_g = gates.reshape(A)
    order = jnp.argsort(flat_e, stable=True)
    e_sorted = flat_e[order]
    counts = jnp.bincount(flat_e, length=N_EXPERTS)
    padded = (counts + MOE_BLOCK - 1) // MOE_BLOCK * MOE_BLOCK
    pad_end = jnp.cumsum(padded)
    pad_start = pad_end - padded
    grp_start = jnp.cumsum(counts) - counts
    dest = pad_start[e_sorted] + jnp.arange(A, dtype=jnp.int32) - grp_start[e_sorted]
    n_blocks = -(-A // MOE_BLOCK) + N_EXPERTS
    P = n_blocks * MOE_BLOCK
    slot_tok = jnp.full((P,), T, dtype=jnp.int32).at[dest].set(flat_t[order])
    slot_gate = jnp.zeros((P,), f32).at[dest].set(flat_g[order])
    block_expert = jnp.minimum(
        jnp.searchsorted(pad_end, jnp.arange(n_blocks, dtype=jnp.int32) * MOE_BLOCK, side='right'),
        N_EXPERTS - 1)
    h_pad = jnp.concatenate([h, jnp.zeros((1, h.shape[1]), h.dtype)], axis=0)

    def expert_block(args):
        toks, gate, e = args
        xe = h_pad[toks]
        act = jax.nn.silu(xe @ w_e_gate[layer, e]) * (xe @ w_e_up[layer, e])
        ye = act @ w_e_down[layer, e]
        return ye * gate[:, None].astype(ye.dtype)

    yb = lax.map(expert_block, (slot_tok.reshape(n_blocks, MOE_BLOCK),
                                slot_gate.reshape(n_blocks, MOE_BLOCK), block_expert))
    routed = jax.ops.segment_sum(yb.reshape(P, -1), slot_tok, num_segments=T + 1)[:T]
    shared = (jax.nn.silu(h @ w_s_gate) * (h @ w_s_up)) @ w_s_down
    return routed + shared


def _trunk_layer(x, p, pos, conv_hist, s_ret, s_c, s_n, s_m, l, wts):
    B, L, _ = x.shape
    f32 = jnp.float32
    z = x @ wts['w_in'][l]

    def heads(a, h):
        return a.reshape(B, L, h, -1).astype(f32)

    rq = _rotary(heads(z[..., OFF_RQ:OFF_RK], H_R), pos)
    rk = _rotary(heads(z[..., OFF_RK:OFF_RV], H_R), pos) * DK_R ** -0.5
    rv = heads(z[..., OFF_RV:OFF_RG], H_R)
    u_ext = jnp.concatenate([conv_hist.astype(z.dtype), z[..., OFF_MQ:OFF_MV]], axis=1)
    qk_m = jax.nn.silu(_causal_conv(u_ext, wts['conv_w'][l], wts['conv_b'][l]))
    new_conv = u_ext[:, -(CONV_W - 1):]
    mq = heads(qk_m[..., :W_M], H_M)
    mk = heads(qk_m[..., W_M:], H_M) * DK_M ** -0.5
    mv = heads(z[..., OFF_MV:OFF_MO], H_M)
    mi = (z[..., OFF_MI:OFF_MF] + wts['b_gate_i'][l]).astype(f32)
    mf = jax.nn.log_sigmoid((z[..., OFF_MF:OFF_GA] + wts['b_gate_f'][l]).astype(f32))
    cl = min(L, CHUNK)
    xs = tuple(_to_chunks(a, cl) for a in (rq, rk, rv, mq, mk, mv, mi, mf))
    carry0 = (s_ret.astype(f32), s_c.astype(f32), s_n.astype(f32), s_m.astype(f32))
    (s_ret1, s_c1, s_n1, s_m1), (yr, ym) = lax.scan(_chunk_step, carry0, xs)
    yr = _from_chunks(yr)
    ym = _from_chunks(ym)
    ret_out = _head_norm(yr, wts['g_ret'][l]) * jax.nn.silu(z[..., OFF_RG:OFF_MQ].astype(f32))
    mls_out = _head_norm(ym, wts['g_mlstm'][l]) * jax.nn.sigmoid(z[..., OFF_MO:OFF_MI].astype(f32))
    br_r = ret_out.astype(x.dtype) @ wts['w_br_r'][l]
    br_m = mls_out.astype(x.dtype) @ wts['w_br_m'][l]
    merged = jax.nn.sigmoid(z[..., OFF_GA:OFF_GB]) * br_r + jax.nn.sigmoid(z[..., OFF_GB:N_IN]) * br_m
    h = _layer_norm(ALPHA * x + merged @ wts['w_out'][l], wts['ln1_g'][l], wts['ln1_b'][l])
    ffn = _moe(h.reshape(B * L, D_MODEL), wts['w_router'][l], wts['b_router'][l],
               wts['w_e_gate'], wts['w_e_up'], wts['w_e_down'],
               wts['w_s_gate'][l], wts['w_s_up'][l], wts['w_s_down'][l], l).reshape(B, L, D_MODEL)
    x2 = _layer_norm(ALPHA * h + ffn, wts['ln2_g'][l], wts['ln2_b'][l])
    out = x2 + jax.nn.sigmoid(x2 @ wts['w_pg'][l]) * (p @ wts['w_pe'][l])
    dt = x.dtype
    return out, (s_ret1.astype(dt), s_c1.astype(dt), s_n1.astype(dt), s_m1.astype(dt), new_conv.astype(dt))


def setup_inputs(seed: int = 0) -> dict:
    key = jax.random.key(seed)
    ks = iter(jax.random.split(key, 40))

    def nrm(shape, s):
        return jax.random.normal(next(ks), shape, jnp.float32) * s

    d = {}
    d['x_prompt'] = nrm((BATCH, SEQ, D_MODEL), 1.0)
    d['x_sample'] = nrm((DEC_BATCH, DEC_SEQ, D_MODEL), 1.0)
    d['state_ret'] = nrm((DEPTH, DEC_BATCH, H_R, DK_R, DV_R), 1.0)
    d['state_mlstm_c'] = nrm((DEPTH, DEC_BATCH, H_M, DK_M, DV_M), 1.0)
    d['state_mlstm_n'] = nrm((DEPTH, DEC_BATCH, H_M, DK_M), 1.0)
    d['state_mlstm_m'] = nrm((DEPTH, DEC_BATCH, H_M), 1.0)
    d['state_conv'] = nrm((DEPTH, DEC_BATCH, CONV_W - 1, 2 * W_M), 1.0)
    d['p_prompt'] = nrm((DEPTH, BATCH, SEQ, PLE_DIM), 1.0)
    d['p_sample'] = nrm((DEPTH, DEC_BATCH, DEC_SEQ, PLE_DIM), 1.0)
    d['w_in'] = nrm((DEPTH, D_MODEL, N_IN), D_MODEL ** -0.5)
    d['b_gate_i'] = nrm((DEPTH, H_M), 0.1)
    d['b_gate_f'] = jnp.linspace(3.0, 6.0, H_M, dtype=jnp.float32) + nrm((DEPTH, H_M), 0.1)
    d['conv_w'] = nrm((DEPTH, CONV_W, 2 * W_M), CONV_W ** -0.5)
    d['conv_b'] = nrm((DEPTH, 2 * W_M), 0.02)
    d['g_ret'] = 1.0 + nrm((DEPTH, W_R), 0.02)
    d['g_mlstm'] = 1.0 + nrm((DEPTH, W_M), 0.02)
    d['w_br_r'] = nrm((DEPTH, W_R, D_MODEL), W_R ** -0.5)
    d['w_br_m'] = nrm((DEPTH, W_M, D_MODEL), W_M ** -0.5)
    d['w_out'] = nrm((DEPTH, D_MODEL, D_MODEL), BETA * D_MODEL ** -0.5)
    d['ln1_g'] = 1.0 + nrm((DEPTH, D_MODEL), 0.02)
    d['ln1_b'] = nrm((DEPTH, D_MODEL), 0.02)
    d['w_router'] = nrm((DEPTH, D_MODEL, N_EXPERTS), D_MODEL ** -0.5)
    d['b_router'] = nrm((DEPTH, N_EXPERTS), 0.01)
    d['w_e_gate'] = nrm((DEPTH, N_EXPERTS, D_MODEL, D_EXPERT), D_MODEL ** -0.5)
    d['w_e_up'] = nrm((DEPTH, N_EXPERTS, D_MODEL, D_EXPERT), D_MODEL ** -0.5)
    d['w_e_down'] = nrm((DEPTH, N_EXPERTS, D_EXPERT, D_MODEL), BETA * D_EXPERT ** -0.5)
    d['w_s_gate'] = nrm((DEPTH, D_MODEL, D_SHARED), D_MODEL ** -0.5)
    d['w_s_up'] = nrm((DEPTH, D_MODEL, D_SHARED), D_MODEL ** -0.5)
    d['w_s_down'] = nrm((DEPTH, D_SHARED, D_MODEL), BETA * D_SHARED ** -0.5)
    d['ln2_g'] = 1.0 + nrm((DEPTH, D_MODEL), 0.02)
    d['ln2_b'] = nrm((DEPTH, D_MODEL), 0.02)
    d['w_pe'] = nrm((DEPTH, PLE_DIM, D_MODEL), PLE_DIM ** -0.5)
    d['w_pg'] = nrm((DEPTH, D_MODEL, D_MODEL), D_MODEL ** -0.5)
    return d


def reference(x_prompt, x_sample, state_ret, state_mlstm_c, state_mlstm_n, state_mlstm_m, state_conv,
              p_prompt, p_sample, w_in, b_gate_i, b_gate_f, conv_w, conv_b, g_ret, g_mlstm,
              w_br_r, w_br_m, w_out, ln1_g, ln1_b, w_router, b_router, w_e_gate, w_e_up, w_e_down,
              w_s_gate, w_s_up, w_s_down, ln2_g, ln2_b, w_pe, w_pg):
    wts = dict(w_in=w_in, b_gate_i=b_gate_i, b_gate_f=b_gate_f, conv_w=conv_w, conv_b=conv_b,
               g_ret=g_ret, g_mlstm=g_mlstm, w_br_r=w_br_r, w_br_m=w_br_m, w_out=w_out,
               ln1_g=ln1_g, ln1_b=ln1_b, w_router=w_router, b_router=b_router,
               w_e_gate=w_e_gate, w_e_up=w_e_up, w_e_down=w_e_down,
               w_s_gate=w_s_gate, w_s_up=w_s_up, w_s_down=w_s_down,
               ln2_g=ln2_g, ln2_b=ln2_b, w_pe=w_pe, w_pg=w_pg)
    pos_prompt = jnp.arange(x_prompt.shape[1], dtype=jnp.int32)
    pos_sample = PAST_LEN + jnp.arange(x_sample.shape[1], dtype=jnp.int32)
    bp = x_prompt.shape[0]
    dt = x_prompt.dtype
    z_conv = jnp.zeros((bp, CONV_W - 1, 2 * W_M), dt)
    z_ret = jnp.zeros((bp, H_R, DK_R, DV_R), dt)
    z_c = jnp.zeros((bp, H_M, DK_M, DV_M), dt)
    z_n = jnp.zeros((bp, H_M, DK_M), dt)
    z_m = jnp.zeros((bp, H_M), dt)
    yp, ys = x_prompt, x_sample
    new_p, new_s = [], []
    for l in range(DEPTH):
        yp, st_p = _trunk_layer(yp, p_prompt[l], pos_prompt, z_conv, z_ret, z_c, z_n, z_m, l, wts)
        ys, st_s = _trunk_layer(ys, p_sample[l], pos_sample, state_conv[l], state_ret[l],
                                state_mlstm_c[l], state_mlstm_n[l], state_mlstm_m[l], l, wts)
        new_p.append(st_p)
        new_s.append(st_s)
    ret_p = jnp.stack([s[0] for s in new_p])
    c_p = jnp.stack([s[1] for s in new_p])
    n_p = jnp.stack([s[2] for s in new_p])
    m_p = jnp.stack([s[3] for s in new_p])
    conv_p = jnp.stack([s[4] for s in new_p])
    ret_s = jnp.stack([s[0] for s in new_s])
    c_s = jnp.stack([s[1] for s in new_s])
    n_s = jnp.stack([s[2] for s in new_s])
    m_s = jnp.stack([s[3] for s in new_s])
    conv_s = jnp.stack([s[4] for s in new_s])
    return (yp, ys, ret_p, c_p, n_p, m_p, conv_p, ret_s, c_s, n_s, m_s, conv_s)
```

```python
import functools

import jax
import jax.numpy as jnp
from jax import lax
from jax.experimental import pallas as pl
from jax.experimental.pallas import tpu as pltpu

F32 = jnp.float32
MXU_DTYPE = jnp.bfloat16

N_HEADS = 8
PAST_LEN = 1024
TOP_K = 8
N_GROUPS = 8
TOPK_GROUPS = 4
ROUTED_SCALE = 2.5
ROPE_BASE = 10000.0
LN_EPS = 1e-5

V7X_VMEM_LIMIT_BYTES = 56 * 1024 * 1024
PROMPT_CHUNK = 256
EXPERT_ROWS = 384
EXPERT_FF_TILE = 256


def _cparams(sem):
    return pltpu.CompilerParams(dimension_semantics=sem, vmem_limit_bytes=V7X_VMEM_LIMIT_BYTES)


def _tile(n, pref):
    if n <= pref:
        return n
    t = pref
    while n % t:
        t //= 2
    return t


def _mx(a):
    return a.astype(MXU_DTYPE)


def _dot(a, b):
    return jnp.dot(_mx(a), _mx(b), preferred_element_type=F32)


def _fused_matmul_kernel(*refs, n_lhs, lhs_of, n_extra, epilogue):
    n_rhs = len(lhs_of)
    lhs = refs[:n_lhs]
    rhs = refs[n_lhs:n_lhs + n_rhs]
    extra = refs[n_lhs + n_rhs:n_lhs + n_rhs + n_extra]
    outs = refs[n_lhs + n_rhs + n_extra:]
    accs = [_dot(lhs[j][...], b[...]) for j, b in zip(lhs_of, rhs)]
    res = epilogue(accs, [e[...] for e in extra])
    if not isinstance(res, (tuple, list)):
        res = (res,)
    for o, r in zip(outs, res):
        o[...] = r.astype(o.dtype)


def _fused_matmul(lhs, rhs, extras, epilogue, out_dtypes, n_cols, *, tm=1024, tn=512, name):
    m = lhs[0].shape[0]
    tm = _tile(m, tm)
    tn = _tile(n_cols, tn)
    in_specs = []
    for a in lhs:
        in_specs.append(pl.BlockSpec((tm, a.shape[1]), lambda n, i: (i, 0)))
    for (b, off, _) in rhs:
        assert off % tn == 0
        in_specs.append(pl.BlockSpec((b.shape[0], tn), functools.partial(lambda n, i, o: (0, n + o), o=off // tn)))
    for (e, off) in extras:
        assert off % tn == 0
        in_specs.append(pl.BlockSpec((tm, tn), functools.partial(lambda n, i, o: (i, n + o), o=off // tn)))
    out_shape = [jax.ShapeDtypeStruct((m, n_cols), dt) for dt in out_dtypes]
    out_specs = [pl.BlockSpec((tm, tn), lambda n, i: (i, n)) for _ in out_dtypes]
    kern = functools.partial(_fused_matmul_kernel, n_lhs=len(lhs), lhs_of=tuple(j for _, _, j in rhs),
                             n_extra=len(extras), epilogue=epilogue)
    res = pl.pallas_call(
        kern, grid=(n_cols // tn, m // tm), in_specs=in_specs, out_specs=out_specs, out_shape=out_shape,
        compiler_params=_cparams(("parallel", "arbitrary")), name=name,
    )(*lhs, *[b for b, _, _ in rhs], *[e for e, _ in extras])
    return res[0] if len(res) == 1 else res


def _rotary(t, cos, sin):
    half = t.shape[-1] // 2
    t1, t2 = t[:, :half], t[:, half:]
    return jnp.concatenate([t1 * cos - t2 * sin, t1 * sin + t2 * cos], axis=-1)


def _head_norm(y, g):
    mu = jnp.mean(y, axis=-1, keepdims=True)
    d = y - mu
    var = jnp.mean(d * d, axis=-1, keepdims=True)
    return d * lax.rsqrt(var + LN_EPS) * g


def _dot_nt(a, b):
    return lax.dot_general(_mx(a), _mx(b), (((1,), (1,)), ((), ())), preferred_element_type=F32)


def _dot_tn(a, b):
    return lax.dot_general(_mx(a), _mx(b), (((0,), (0,)), ((), ())), preferred_element_type=F32)


def _recurrence_kernel(rq_ref, rk_ref, rv_ref, rg_ref, mq_ref, mk_ref, mv_ref, mo_ref, cos_ref, sin_ref,
                       gcol_ref, grow_ref, dec_ref, qd_ref, kd_ref, sl_ref, gr_ref, gm_ref,
                       s_in, c_in, n_in, m_in,
                       yr_ref, ym_ref, s_ref, c_ref, n_ref, m_ref, *, qk_scale):
    head = pl.program_id(1)
    chunk = pl.program_id(2)
    cl = rq_ref.shape[0]

    @pl.when(chunk == 0)
    def _():
        s_ref[...] = s_in[...]
        c_ref[...] = c_in[...]
        n_ref[...] = n_in[...]
        m_ref[...] = m_in[...]

    cos = cos_ref[...]
    sin = sin_ref[...]

    q = _rotary(rq_ref[...], cos, sin)
    k = _rotary(rk_ref[...], cos, sin) * qk_scale
    v = rv_ref[...]
    s_old = s_ref[...]
    scores = _dot_nt(q, k) * dec_ref[...]
    yr = _dot(scores, v) + _dot(q, s_old) * qd_ref[...]
    s_ref[...] = sl_ref[...] * s_old + _dot_tn(k * kd_ref[...], v)
    yr_ref[...] = (_head_norm(yr, gr_ref[...]) * jax.nn.silu(rg_ref[...])).astype(yr_ref.dtype)

    lane = lax.broadcasted_iota(jnp.int32, gcol_ref.shape, 1)
    gcol = gcol_ref[...]
    b_col = jnp.sum(jnp.where(lane == head, gcol, 0.0), axis=-1, keepdims=True)
    i_col = jnp.sum(jnp.where(lane == head + N_HEADS, gcol, 0.0), axis=-1, keepdims=True)
    b_row = grow_ref[0:1, :]
    i_row = grow_ref[1:2, :]
    m_prev = m_ref[0:1, 0:1]
    t_idx = lax.broadcasted_iota(jnp.int32, (cl, cl), 0)
    s_idx = lax.broadcasted_iota(jnp.int32, (cl, cl), 1)
    dmat = jnp.where(t_idx >= s_idx, b_col - b_row + i_row, -jnp.inf)
    a_col = b_col + m_prev
    m_t = jnp.maximum(a_col, jnp.max(dmat, axis=-1, keepdims=True))
    w_inter = jnp.exp(a_col - m_t)
    w_intra = jnp.exp(dmat - m_t)
    mq = mq_ref[...]
    mk = mk_ref[...] * qk_scale
    mv = mv_ref[...]
    c_old = c_ref[...]
    n_old = n_ref[...]
    qk = _dot_nt(mq, mk) * w_intra
    num = _dot(qk, mv) + w_inter * _dot(mq, c_old)
    den = jnp.sum(qk, axis=-1, keepdims=True) + w_inter * jnp.sum(mq * n_old, axis=-1, keepdims=True)
    ym = num / jnp.maximum(jnp.abs(den), jnp.exp(-m_t))
    m_last = m_t[cl - 1:cl, :]
    wl_inter = w_inter[cl - 1:cl, :]
    wl_col = jnp.exp(b_col[cl - 1:cl, :] - b_col + i_col - m_last)
    c_ref[...] = wl_inter * c_old + _dot_tn(mk * wl_col, mv)
    n_ref[...] = wl_inter * n_old + jnp.sum(mk * wl_col, axis=0, keepdims=True)
    m_ref[...] = jnp.broadcast_to(m_last, m_ref.shape)
    ym_ref[...] = (_head_norm(ym, gm_ref[...]) * jax.nn.sigmoid(mo_ref[...])).astype(ym_ref.dtype)


def _recurrences(z_main, qk_m, cos, sin, gcol, grow, g_ret, g_mlstm, s0, c0, n0, m0, *, row0, n_seq, seq_len, cl,
                 offs, name):
    hd = s0.shape[-1]
    w = N_HEADS * hd
    n_chunks = seq_len // cl
    rb0 = row0 // cl
    cb = {k: v // hd for k, v in offs.items()}

    def rows(b, c):
        return rb0 + b * n_chunks + c

    def zspec(col_key):
        return pl.BlockSpec((cl, hd), functools.partial(lambda b, h, c, o: (rows(b, c), h + o), o=cb[col_key]))

    log_g = jnp.log1p(-jnp.exp2(-5.0 - jnp.arange(N_HEADS, dtype=F32)))
    t = jnp.arange(cl, dtype=F32)
    diff = t[:, None] - t[None, :]
    dec = jnp.exp(jnp.where((diff >= 0)[None], diff[None] * log_g[:, None, None], -jnp.inf))
    qd = jnp.broadcast_to(jnp.exp((t[None, :] + 1.0) * log_g[:, None])[:, :, None], (N_HEADS, cl, hd))
    kd = jnp.broadcast_to(jnp.exp((cl - 1.0 - t)[None, :] * log_g[:, None])[:, :, None], (N_HEADS, cl, hd))
    sl = jnp.broadcast_to(jnp.exp(cl * log_g)[:, None, None], (N_HEADS, 1, hd))

    head_const = lambda shape: pl.BlockSpec((None,) + shape, lambda b, h, c: (h, 0, 0))
    state = lambda shape: pl.BlockSpec((None, None) + shape, lambda b, h, c: (b, h, 0, 0))
    in_specs = [
        zspec("rq"), zspec("rk"), zspec("rv"), zspec("rg"),
        pl.BlockSpec((cl, hd), lambda b, h, c: (rows(b, c), h)),
        pl.BlockSpec((cl, hd), lambda b, h, c: (rows(b, c), h + N_HEADS)),
        zspec("mv"), zspec("mo"),
        pl.BlockSpec((cl, hd // 2), lambda b, h, c: (rows(b, c), 0)),
        pl.BlockSpec((cl, hd // 2), lambda b, h, c: (rows(b, c), 0)),
        pl.BlockSpec((cl, 128), lambda b, h, c: (rows(b, c), 0)),
        pl.BlockSpec((None, None, 8, cl), lambda b, h, c: (b * n_chunks + c, h, 0, 0)),
        head_const((cl, cl)), head_const((cl, hd)), head_const((cl, hd)), head_const((1, hd)),
        pl.BlockSpec((1, hd), lambda b, h, c: (0, h)), pl.BlockSpec((1, hd), lambda b, h, c: (0, h)),
        state((hd, hd)), state((hd, hd)), state((1, hd)), state((1, 128)),
    ]
    n_rows = n_seq * seq_len
    out_shape = [
        jax.ShapeDtypeStruct((n_rows, w), MXU_DTYPE), jax.ShapeDtypeStruct((n_rows, w), MXU_DTYPE),
        jax.ShapeDtypeStruct((n_seq, N_HEADS, hd, hd), F32), jax.ShapeDtypeStruct((n_seq, N_HEADS, hd, hd), F32),
        jax.ShapeDtypeStruct((n_seq, N_HEADS, 1, hd), F32), jax.ShapeDtypeStruct((n_seq, N_HEADS, 1, 128), F32),
    ]
    y_spec = pl.BlockSpec((cl, hd), lambda b, h, c: (b * n_chunks + c, h))
    out_specs = [y_spec, y_spec, state((hd, hd)), state((hd, hd)), state((1, hd)), state((1, 128))]
    kern = functools.partial(_recurrence_kernel, qk_scale=float(hd) ** -0.5)
    return pl.pallas_call(
        kern, grid=(n_seq, N_HEADS, n_chunks), in_specs=in_specs, out_specs=out_specs, out_shape=out_shape,
        compiler_params=_cparams(("parallel", "parallel", "arbitrary")), name=name,
    )(z_main, z_main, z_main, z_main, qk_m, qk_m, z_main, z_main, cos, sin, gcol, grow, dec, qd, kd, sl,
      g_ret, g_mlstm, s0, c0, n0, m0)


def _layer_norm_rows(u, g, b):
    mu = jnp.mean(u, axis=-1, keepdims=True)
    d = u - mu
    var = jnp.mean(d * d, axis=-1, keepdims=True)
    return d * lax.rsqrt(var + LN_EPS) * g + b


def _ln_router_kernel(u_ref, g_ref, b_ref, wr_ref, h_ref, hb_ref, logit_ref):
    h = _layer_norm_rows(u_ref[...], g_ref[...], b_ref[...])
    h_ref[...] = h
    hb = _mx(h)
    hb_ref[...] = hb
    logit_ref[...] = jnp.dot(hb, _mx(wr_ref[...]), preferred_element_type=F32)


def _ln_router(u, g, b, w_router):
    t, d = u.shape
    e = w_router.shape[1]
    tr = _tile(t, 256)
    row = pl.BlockSpec((tr, d), lambda i: (i, 0))
    vec = pl.BlockSpec((1, d), lambda i: (0, 0))
    return pl.pallas_call(
        _ln_router_kernel, grid=(t // tr,),
        in_specs=[row, vec, vec, pl.BlockSpec((d, e), lambda i: (0, 0))],
        out_specs=[row, row, pl.BlockSpec((tr, e), lambda i: (i, 0))],
        out_shape=[jax.ShapeDtypeStruct((t, d), F32), jax.ShapeDtypeStruct((t, d), MXU_DTYPE),
                   jax.ShapeDtypeStruct((t, e), F32)],
        compiler_params=_cparams(("parallel",)), name="ln1_router",
    )(u, g, b, w_router)


def _ln_kernel(u_ref, g_ref, b_ref, x_ref, xb_ref):
    x2 = _layer_norm_rows(u_ref[...], g_ref[...], b_ref[...])
    x_ref[...] = x2
    xb_ref[...] = _mx(x2)


def _ln(u, g, b):
    t, d = u.shape
    tr = _tile(t, 256)
    row = pl.BlockSpec((tr, d), lambda i: (i, 0))
    vec = pl.BlockSpec((1, d), lambda i: (0, 0))
    return pl.pallas_call(
        _ln_kernel, grid=(t // tr,), in_specs=[row, vec, vec], out_specs=[row, row],
        out_shape=[jax.ShapeDtypeStruct((t, d), F32), jax.ShapeDtypeStruct((t, d), MXU_DTYPE)],
        compiler_params=_cparams(("parallel",)), name="ln2",
    )(u, g, b)


def _expert_kernel(be_ref, nu_ref, x_ref, wg_ref, wu_ref, wd_ref, y_ref, acc_ref):
    blk = pl.program_id(0)
    f = pl.program_id(1)

    @pl.when(blk < nu_ref[0])
    def _():
        x = x_ref[...]
        g = jnp.dot(x, _mx(wg_ref[...]), preferred_element_type=F32)
        u = jnp.dot(x, _mx(wu_ref[...]), preferred_element_type=F32)
        part = _dot(jax.nn.silu(g) * u, wd_ref[...])

        @pl.when(f == 0)
        def _():
            acc_ref[...] = part

        @pl.when(f > 0)
        def _():
            acc_ref[...] += part

        @pl.when(f == pl.num_programs(1) - 1)
        def _():
            y_ref[...] = acc_ref[...].astype(y_ref.dtype)


def _grouped_experts(x_sorted, block_expert, n_used, w_gate, w_up, w_down):
    p, d = x_sorted.shape
    n_blocks = p // EXPERT_ROWS
    dff = w_gate.shape[-1]
    tf = _tile(dff, EXPERT_FF_TILE)
    nf = dff // tf

    def live(b, nu):
        return jnp.minimum(b, nu[0] - 1)

    def ff(b, f, nu):
        return jnp.where(b < nu[0], f, nf - 1)

    grid_spec = pltpu.PrefetchScalarGridSpec(
        num_scalar_prefetch=2, grid=(n_blocks, nf),
        in_specs=[
            pl.BlockSpec((EXPERT_ROWS, d), lambda b, f, be, nu: (live(b, nu), 0)),
            pl.BlockSpec((None, d, tf), lambda b, f, be, nu: (be[live(b, nu)], 0, ff(b, f, nu))),
            pl.BlockSpec((None, d, tf), lambda b, f, be, nu: (be[live(b, nu)], 0, ff(b, f, nu))),
            pl.BlockSpec((None, tf, d), lambda b, f, be, nu: (be[live(b, nu)], ff(b, f, nu), 0)),
        ],
        out_specs=pl.BlockSpec((EXPERT_ROWS, d), lambda b, f, be, nu: (live(b, nu), 0)),
        scratch_shapes=[pltpu.VMEM((EXPERT_ROWS, d), F32)],
    )
    return pl.pallas_call(
        _expert_kernel, grid_spec=grid_spec, out_shape=jax.ShapeDtypeStruct((p, d), MXU_DTYPE),
        compiler_params=_cparams(("arbitrary", "arbitrary")), name="grouped_experts",
    )(block_expert, n_used, x_sorted, w_gate, w_up, w_down)


def _route(logits, b_router):
    t, e = logits.shape
    scores = jax.nn.sigmoid(logits)
    sel = scores + b_router.astype(F32)
    grp_score = lax.top_k(sel.reshape(t, N_GROUPS, -1), 2)[0].sum(-1)
    _, grp_idx = lax.top_k(grp_score, TOPK_GROUPS)
    grp_mask = jax.nn.one_hot(grp_idx, N_GROUPS, dtype=F32).sum(1) > 0
    sel = jnp.where(jnp.repeat(grp_mask, e // N_GROUPS, axis=1), sel, -jnp.inf)
    _, top_idx = lax.top_k(sel, TOP_K)
    top_s = jnp.take_along_axis(scores, top_idx, axis=1)
    gates = top_s / top_s.sum(-1, keepdims=True) * ROUTED_SCALE
    return top_idx.astype(jnp.int32), gates


def _routed_experts(hb, logits, b_router, w_gate, w_up, w_down):
    t, d = hb.shape
    e = w_gate.shape[0]
    top_idx, gates = _route(logits, b_router)
    a = t * TOP_K
    flat_e = top_idx.reshape(a)
    order = jnp.argsort(flat_e, stable=True).astype(jnp.int32)
    e_sorted = flat_e[order]
    counts = jnp.bincount(flat_e, length=e).astype(jnp.int32)
    blocks_per = (counts + EXPERT_ROWS - 1) // EXPERT_ROWS
    blk_end = jnp.cumsum(blocks_per)
    pad_start = (blk_end - blocks_per) * EXPERT_ROWS
    grp_start = jnp.cumsum(counts) - counts
    dest = pad_start[e_sorted] + jnp.arange(a, dtype=jnp.int32) - grp_start[e_sorted]
    n_blocks = -(-a // EXPERT_ROWS) + e
    p = n_blocks * EXPERT_ROWS
    slot_tok = jnp.zeros((p,), jnp.int32).at[dest].set(order // TOP_K)
    pos = jnp.zeros((a,), jnp.int32).at[order].set(dest)
    block_expert = jnp.minimum(
        jnp.searchsorted(blk_end, jnp.arange(n_blocks, dtype=jnp.int32), side="right"), e - 1).astype(jnp.int32)
    n_used = blk_end[-1:].astype(jnp.int32)
    x_sorted = hb[slot_tok]
    y = _grouped_experts(x_sorted, block_expert, n_used, w_gate, w_up, w_down)
    yk = y[pos].reshape(t, TOP_K, d).astype(F32)
    return jnp.sum(yk * gates[:, :, None], axis=1)


def kernel(x_prompt, x_sample, state_ret, state_mlstm_c, state_mlstm_n, state_mlstm_m, state_conv, p_prompt, p_sample, w_in, b_gate_i, b_gate_f, conv_w, conv_b, g_ret, g_mlstm, w_br_r, w_br_m, w_out, ln1_g, ln1_b, w_router, b_router, w_e_gate, w_e_up, w_e_down, w_s_gate, w_s_up, w_s_down, ln2_g, ln2_b, w_pe, w_pg):
    depth = w_in.shape[0]
    assert depth == 1, "single-layer step"
    alpha = (2 * depth) ** 0.25
    bp, lp, d = x_prompt.shape
    bs, ls, _ = x_sample.shape
    assert bp == 1
    hd = state_ret.shape[-1]
    w = N_HEADS * hd
    conv_hist = state_conv.shape[2]
    off = {"rq": 0, "rk": w, "rv": 2 * w, "rg": 3 * w, "mq": 4 * w, "mk": 5 * w, "mv": 6 * w, "mo": 7 * w}
    off_mi = 8 * w
    off_ga = off_mi + 2 * N_HEADS
    n_main = 8 * w
    tp, ts = bp * lp, bs * ls
    t = tp + ts

    x = jnp.concatenate([x_prompt.reshape(tp, d), x_sample.reshape(ts, d)], axis=0)
    xb = _mx(x)
    w_in0 = w_in[0]
    first = lambda accs, extras: accs[0]

    z_main = _fused_matmul([xb], [(w_in0, 0, 0)], [], first, [F32], n_main, name="proj_main")
    w_if = jnp.pad(w_in0[:, off_mi:off_ga], ((0, 0), (0, 128 - 2 * N_HEADS)))
    z_if = _fused_matmul([xb], [(w_if, 0, 0)], [], first, [F32], 128, name="proj_gates")
    w_gab = w_in0[:, off_ga:]
    z_gab = _fused_matmul([xb], [(w_gab, 0, 0)], [], first, [F32], 2 * d, name="proj_merge_gates")

    u = z_main[:, off["mq"]:off["mv"]]
    u_p = jnp.concatenate([jnp.zeros((bp, conv_hist, 2 * w), F32), u[:tp].reshape(bp, lp, 2 * w)], axis=1)
    u_s = jnp.concatenate([state_conv[0], u[tp:].reshape(bs, ls, 2 * w)], axis=1)

    def conv(u_ext, length):
        acc = conv_b[0]
        for j in range(conv_hist + 1):
            acc = acc + conv_w[0, j] * u_ext[:, j:j + length]
        return jax.nn.silu(acc)

    qk_m = jnp.concatenate([conv(u_p, lp).reshape(tp, 2 * w), conv(u_s, ls).reshape(ts, 2 * w)], axis=0)
    conv_p = u_p[:, -conv_hist:][None]
    conv_s = u_s[:, -conv_hist:][None]

    pos = jnp.concatenate([jnp.tile(jnp.arange(lp, dtype=jnp.int32), bp),
                           jnp.tile(PAST_LEN + jnp.arange(ls, dtype=jnp.int32), bs)])
    half = hd // 2
    inv_freq = ROPE_BASE ** (-jnp.arange(half, dtype=F32) / half)
    ang = pos.astype(F32)[:, None] * inv_freq[None, :]
    cos, sin = jnp.cos(ang), jnp.sin(ang)

    mi = z_if[:, :N_HEADS] + b_gate_i[0]
    lf = jax.nn.log_sigmoid(z_if[:, N_HEADS:2 * N_HEADS] + b_gate_f[0])
    cl_p = _tile(lp, PROMPT_CHUNK)
    cl_s = ls

    def gate_layouts(rows, cl):
        b = jnp.cumsum(lf[rows].reshape(-1, cl, N_HEADS), axis=1)
        i = mi[rows].reshape(-1, cl, N_HEADS)
        col = jnp.concatenate([b, i], axis=-1).reshape(-1, 2 * N_HEADS)
        row = jnp.stack([jnp.swapaxes(b, 1, 2), jnp.swapaxes(i, 1, 2)], axis=2)
        return jnp.pad(col, ((0, 0), (0, 128 - 2 * N_HEADS))), jnp.pad(row, ((0, 0), (0, 0), (0, 6), (0, 0)))

    gcol_p, grow_p = gate_layouts(slice(0, tp), cl_p)
    gcol_s, grow_s = gate_layouts(slice(tp, t), cl_s)
    gcol = jnp.concatenate([gcol_p, gcol_s], axis=0)

    zeros = lambda *shape: jnp.zeros(shape, F32)
    rec = functools.partial(_recurrences, z_main, qk_m, cos, sin, gcol, g_ret=g_ret, g_mlstm=g_mlstm, offs=off)
    yr_p, ym_p, s_p, c_p, n_p, m_p = rec(
        grow_p, s0=zeros(bp, N_HEADS, hd, hd), c0=zeros(bp, N_HEADS, hd, hd),
        n0=zeros(bp, N_HEADS, 1, hd), m0=zeros(bp, N_HEADS, 1, 128), row0=0, n_seq=bp, seq_len=lp, cl=cl_p,
        name="recurrence_prompt")
    m0_s = jnp.broadcast_to(state_mlstm_m[0][:, :, None, None], (bs, N_HEADS, 1, 128))
    yr_s, ym_s, s_s, c_s, n_s, m_s = rec(
        grow_s, s0=state_ret[0], c0=state_mlstm_c[0], n0=state_mlstm_n[0][:, :, None, :], m0=m0_s,
        row0=tp, n_seq=bs, seq_len=ls, cl=cl_s, name="recurrence_sample")
    ret_out = jnp.concatenate([yr_p, yr_s], axis=0)
    mls_out = jnp.concatenate([ym_p, ym_s], axis=0)

    def merge(accs, extras):
        return jax.nn.sigmoid(extras[0]) * accs[0] + jax.nn.sigmoid(extras[1]) * accs[1]

    merged = _fused_matmul([ret_out, mls_out], [(w_br_r[0], 0, 0), (w_br_m[0], 0, 1)], [(z_gab, 0), (z_gab, d)],
                           merge, [MXU_DTYPE], d, tm=512, name="merge")

    u1 = _fused_matmul([merged], [(w_out[0], 0, 0)], [(x, 0)], lambda accs, extras: alpha * extras[0] + accs[0],
                       [F32], d, name="out_proj")
    h, hb, logits = _ln_router(u1, ln1_g, ln1_b, w_router[0])

    routed = _routed_experts(hb, logits, b_router[0], w_e_gate[0], w_e_up[0], w_e_down[0])
    act = _fused_matmul([hb], [(w_s_gate[0], 0, 0), (w_s_up[0], 0, 0)], [],
                        lambda accs, extras: jax.nn.silu(accs[0]) * accs[1], [MXU_DTYPE], w_s_gate.shape[-1],
                        tm=512, tn=256, name="shared_up")
    u2 = _fused_matmul([act], [(w_s_down[0], 0, 0)], [(h, 0), (routed, 0)],
                       lambda accs, extras: alpha * extras[0] + (extras[1] + accs[0]), [F32], d, name="shared_down")

    x2, x2b = _ln(u2, ln2_g, ln2_b)
    p = _mx(jnp.concatenate([p_prompt[0].reshape(tp, -1), p_sample[0].reshape(ts, -1)], axis=0))
    out = _fused_matmul([x2b, p], [(w_pg[0], 0, 0), (w_pe[0], 0, 1)], [(x2, 0)],
                        lambda accs, extras: extras[0] + jax.nn.sigmoid(accs[0]) * accs[1], [F32], d, tm=512,
                        name="embed_gate")

    dt = x_prompt.dtype
    y_p = out[:tp].reshape(bp, lp, d).astype(dt)
    y_s = out[tp:].reshape(bs, ls, d).astype(dt)
    return (y_p, y_s,
            s_p[None].astype(dt), c_p[None].astype(dt), n_p[:, :, 0][None].astype(dt), m_p[:, :, 0, 0][None].astype(dt),
            conv_p.astype(dt),
            s_s[None].astype(dt), c_s[None].astype(dt), n_s[:, :, 0][None].astype(dt), m_s[:, :, 0, 0][None].astype(dt),
            conv_s.astype(dt))
```

```python
import functools

import jax
import jax.numpy as jnp
from jax import lax
from jax.experimental import pallas as pl
from jax.experimental.pallas import tpu as pltpu

F32 = jnp.float32
MXU_DTYPE = jnp.bfloat16

N_HEADS = 8
PAST_LEN = 1024
TOP_K = 8
N_GROUPS = 8
TOPK_GROUPS = 4
ROUTED_SCALE = 2.5
ROPE_BASE = 10000.0
LN_EPS = 1e-5

V7X_VMEM_LIMIT_BYTES = 56 * 1024 * 1024
PROMPT_CHUNK = 256
EXPERT_ROWS = 384
EXPERT_FF_TILE = 256


def _cparams(sem):
    return pltpu.CompilerParams(dimension_semantics=sem, vmem_limit_bytes=V7X_VMEM_LIMIT_BYTES)


def _tile(n, pref):
    if n <= pref:
        return n
    t = pref
    while n % t:
        t //= 2
    return t


def _mx(a):
    return a.astype(MXU_DTYPE)


def _dot(a, b):
    return jnp.dot(_mx(a), _mx(b), preferred_element_type=F32)


def _fused_matmul_kernel(*refs, n_lhs, lhs_of, n_extra, epilogue):
    n_rhs = len(lhs_of)
    lhs = refs[:n_lhs]
    rhs = refs[n_lhs:n_lhs + n_rhs]
    extra = refs[n_lhs + n_rhs:n_lhs + n_rhs + n_extra]
    outs = refs[n_lhs + n_rhs + n_extra:]
    accs = [_dot(lhs[j][...], b[...]) for j, b in zip(lhs_of, rhs)]
    res = epilogue(accs, [e[...] for e in extra])
    if not isinstance(res, (tuple, list)):
        res = (res,)
    for o, r in zip(outs, res):
        o[...] = r.astype(o.dtype)


def _fused_matmul(lhs, rhs, extras, epilogue, out_dtypes, n_cols, *, tm=1024, tn=512, name):
    m = lhs[0].shape[0]
    tm = _tile(m, tm)
    tn = _tile(n_cols, tn)
    in_specs = []
    for a in lhs:
        in_specs.append(pl.BlockSpec((tm, a.shape[1]), lambda n, i: (i, 0)))
    for (b, off, _) in rhs:
        assert off % tn == 0
        in_specs.append(pl.BlockSpec((b.shape[0], tn), functools.partial(lambda n, i, o: (0, n + o), o=off // tn)))
    for (e, off) in extras:
        if off is None:
            in_specs.append(pl.BlockSpec((tm, e.shape[1]), lambda n, i: (i, 0)))
        else:
            assert off % tn == 0
            in_specs.append(pl.BlockSpec((tm, tn), functools.partial(lambda n, i, o: (i, n + o), o=off // tn)))
    out_shape = [jax.ShapeDtypeStruct((m, n_cols), dt) for dt in out_dtypes]
    out_specs = [pl.BlockSpec((tm, tn), lambda n, i: (i, n)) for _ in out_dtypes]
    kern = functools.partial(_fused_matmul_kernel, n_lhs=len(lhs), lhs_of=tuple(j for _, _, j in rhs),
                             n_extra=len(extras), epilogue=epilogue)
    res = pl.pallas_call(
        kern, grid=(n_cols // tn, m // tm), in_specs=in_specs, out_specs=out_specs, out_shape=out_shape,
        compiler_params=_cparams(("parallel", "arbitrary")), name=name,
    )(*lhs, *[b for b, _, _ in rhs], *[e for e, _ in extras])
    return res[0] if len(res) == 1 else res


def _rotary(t, cos, sin):
    half = t.shape[-1] // 2
    t1, t2 = t[:, :half], t[:, half:]
    return jnp.concatenate([t1 * cos - t2 * sin, t1 * sin + t2 * cos], axis=-1)


def _head_norm(y, g):
    mu = jnp.mean(y, axis=-1, keepdims=True)
    d = y - mu
    var = jnp.mean(d * d, axis=-1, keepdims=True)
    return d * lax.rsqrt(var + LN_EPS) * g


def _dot_nt(a, b):
    return lax.dot_general(_mx(a), _mx(b), (((1,), (1,)), ((), ())), preferred_element_type=F32)


def _dot_tn(a, b):
    return lax.dot_general(_mx(a), _mx(b), (((0,), (0,)), ((), ())), preferred_element_type=F32)


def _recurrence_kernel(rq_ref, rk_ref, rv_ref, rg_ref, mq_ref, mk_ref, mv_ref, mo_ref, cos_ref, sin_ref,
                       gcol_ref, grow_ref, dec_ref, qd_ref, kd_ref, sl_ref, gr_ref, gm_ref,
                       s_in, c_in, n_in, m_in,
                       yr_ref, ym_ref, s_ref, c_ref, n_ref, m_ref, *, qk_scale):
    head = pl.program_id(1)
    chunk = pl.program_id(2)
    cl = rq_ref.shape[0]

    @pl.when(chunk == 0)
    def _():
        s_ref[...] = s_in[...]
        c_ref[...] = c_in[...]
        n_ref[...] = n_in[...]
        m_ref[...] = m_in[...]

    cos = cos_ref[...]
    sin = sin_ref[...]

    q = _rotary(rq_ref[...], cos, sin)
    k = _rotary(rk_ref[...], cos, sin) * qk_scale
    v = rv_ref[...]
    s_old = s_ref[...]
    scores = _dot_nt(q, k) * dec_ref[...]
    yr = _dot(scores, v) + _dot(q, s_old) * qd_ref[...]
    s_ref[...] = sl_ref[...] * s_old + _dot_tn(k * kd_ref[...], v)
    yr_ref[...] = (_head_norm(yr, gr_ref[...]) * jax.nn.silu(rg_ref[...])).astype(yr_ref.dtype)

    lane = lax.broadcasted_iota(jnp.int32, gcol_ref.shape, 1)
    gcol = gcol_ref[...]
    b_col = jnp.sum(jnp.where(lane == head, gcol, 0.0), axis=-1, keepdims=True)
    i_col = jnp.sum(jnp.where(lane == head + N_HEADS, gcol, 0.0), axis=-1, keepdims=True)
    b_row = grow_ref[0:1, :]
    i_row = grow_ref[1:2, :]
    m_prev = m_ref[0:1, 0:1]
    t_idx = lax.broadcasted_iota(jnp.int32, (cl, cl), 0)
    s_idx = lax.broadcasted_iota(jnp.int32, (cl, cl), 1)
    dmat = jnp.where(t_idx >= s_idx, b_col - b_row + i_row, -jnp.inf)
    a_col = b_col + m_prev
    m_t = jnp.maximum(a_col, jnp.max(dmat, axis=-1, keepdims=True))
    w_inter = jnp.exp(a_col - m_t)
    w_intra = jnp.exp(dmat - m_t)
    mq = mq_ref[...]
    mk = mk_ref[...] * qk_scale
    mv = mv_ref[...]
    c_old = c_ref[...]
    n_old = n_ref[...]
    qk = _dot_nt(mq, mk) * w_intra
    num = _dot(qk, mv) + w_inter * _dot(mq, c_old)
    den = jnp.sum(qk, axis=-1, keepdims=True) + w_inter * jnp.sum(mq * n_old, axis=-1, keepdims=True)
    ym = num / jnp.maximum(jnp.abs(den), jnp.exp(-m_t))
    m_last = m_t[cl - 1:cl, :]
    wl_inter = w_inter[cl - 1:cl, :]
    wl_col = jnp.exp(b_col[cl - 1:cl, :] - b_col + i_col - m_last)
    c_ref[...] = wl_inter * c_old + _dot_tn(mk * wl_col, mv)
    n_ref[...] = wl_inter * n_old + jnp.sum(mk * wl_col, axis=0, keepdims=True)
    m_ref[...] = jnp.broadcast_to(m_last, m_ref.shape)
    ym_ref[...] = (_head_norm(ym, gm_ref[...]) * jax.nn.sigmoid(mo_ref[...])).astype(ym_ref.dtype)


def _recurrences(z_main, qk_m, cos, sin, gcol, grow, g_ret, g_mlstm, s0, c0, n0, m0, *, row0, n_seq, seq_len, cl,
                 offs, name):
    hd = s0.shape[-1]
    w = N_HEADS * hd
    n_chunks = seq_len // cl
    rb0 = row0 // cl
    cb = {k: v // hd for k, v in offs.items()}

    def rows(b, c):
        return rb0 + b * n_chunks + c

    def zspec(col_key):
        return pl.BlockSpec((cl, hd), functools.partial(lambda b, h, c, o: (rows(b, c), h + o), o=cb[col_key]))

    log_g = jnp.log1p(-jnp.exp2(-5.0 - jnp.arange(N_HEADS, dtype=F32)))
    t = jnp.arange(cl, dtype=F32)
    diff = t[:, None] - t[None, :]
    dec = jnp.exp(jnp.where((diff >= 0)[None], diff[None] * log_g[:, None, None], -jnp.inf))
    qd = jnp.broadcast_to(jnp.exp((t[None, :] + 1.0) * log_g[:, None])[:, :, None], (N_HEADS, cl, hd))
    kd = jnp.broadcast_to(jnp.exp((cl - 1.0 - t)[None, :] * log_g[:, None])[:, :, None], (N_HEADS, cl, hd))
    sl = jnp.broadcast_to(jnp.exp(cl * log_g)[:, None, None], (N_HEADS, 1, hd))

    head_const = lambda shape: pl.BlockSpec((None,) + shape, lambda b, h, c: (h, 0, 0))
    state = lambda shape: pl.BlockSpec((None, None) + shape, lambda b, h, c: (b, h, 0, 0))
    in_specs = [
        zspec("rq"), zspec("rk"), zspec("rv"), zspec("rg"),
        pl.BlockSpec((cl, hd), lambda b, h, c: (rows(b, c), h)),
        pl.BlockSpec((cl, hd), lambda b, h, c: (rows(b, c), h + N_HEADS)),
        zspec("mv"), zspec("mo"),
        pl.BlockSpec((cl, hd // 2), lambda b, h, c: (rows(b, c), 0)),
        pl.BlockSpec((cl, hd // 2), lambda b, h, c: (rows(b, c), 0)),
        pl.BlockSpec((cl, 128), lambda b, h, c: (rows(b, c), 0)),
        pl.BlockSpec((None, None, 8, cl), lambda b, h, c: (b * n_chunks + c, h, 0, 0)),
        head_const((cl, cl)), head_const((cl, hd)), head_const((cl, hd)), head_const((1, hd)),
        pl.BlockSpec((1, hd), lambda b, h, c: (0, h)), pl.BlockSpec((1, hd), lambda b, h, c: (0, h)),
        state((hd, hd)), state((hd, hd)), state((1, hd)), state((1, 128)),
    ]
    n_rows = n_seq * seq_len
    out_shape = [
        jax.ShapeDtypeStruct((n_rows, w), MXU_DTYPE), jax.ShapeDtypeStruct((n_rows, w), MXU_DTYPE),
        jax.ShapeDtypeStruct((n_seq, N_HEADS, hd, hd), F32), jax.ShapeDtypeStruct((n_seq, N_HEADS, hd, hd), F32),
        jax.ShapeDtypeStruct((n_seq, N_HEADS, 1, hd), F32), jax.ShapeDtypeStruct((n_seq, N_HEADS, 1, 128), F32),
    ]
    y_spec = pl.BlockSpec((cl, hd), lambda b, h, c: (b * n_chunks + c, h))
    out_specs = [y_spec, y_spec, state((hd, hd)), state((hd, hd)), state((1, hd)), state((1, 128))]
    kern = functools.partial(_recurrence_kernel, qk_scale=float(hd) ** -0.5)
    return pl.pallas_call(
        kern, grid=(n_seq, N_HEADS, n_chunks), in_specs=in_specs, out_specs=out_specs, out_shape=out_shape,
        compiler_params=_cparams(("parallel", "parallel", "arbitrary")), name=name,
    )(z_main, z_main, z_main, z_main, qk_m, qk_m, z_main, z_main, cos, sin, gcol, grow, dec, qd, kd, sl,
      g_ret, g_mlstm, s0, c0, n0, m0)


def _layer_norm_rows(u, g, b):
    mu = jnp.mean(u, axis=-1, keepdims=True)
    d = u - mu
    var = jnp.mean(d * d, axis=-1, keepdims=True)
    return d * lax.rsqrt(var + LN_EPS) * g + b


def _first_argmax(vals, lane_f, n):
    mx = jnp.max(vals, axis=-1, keepdims=True)
    return mx, jnp.min(jnp.where(vals == mx, lane_f, float(n)), axis=-1, keepdims=True)


def _route_rows(logits, bias):
    rows, e = logits.shape
    gsz = e // N_GROUPS
    lane = lax.broadcasted_iota(jnp.int32, (rows, e), 1)
    lane_f = lane.astype(F32)
    scores = jax.nn.sigmoid(logits)
    sel = scores + bias
    in_grp = [(lane >= g * gsz) & (lane < (g + 1) * gsz) for g in range(N_GROUPS)]
    grp_score = []
    for g in range(N_GROUPS):
        m = jnp.where(in_grp[g], sel, -jnp.inf)
        m1, i1 = _first_argmax(m, lane_f, e)
        m2 = jnp.max(jnp.where(lane_f == i1, -jnp.inf, m), axis=-1, keepdims=True)
        grp_score.append(m1 + m2)
    keep = jnp.zeros((rows, e), F32)
    for g in range(N_GROUPS):
        rank = jnp.zeros_like(grp_score[g])
        for o in range(N_GROUPS):
            if o != g:
                ahead = (grp_score[o] >= grp_score[g]) if o < g else (grp_score[o] > grp_score[g])
                rank = rank + ahead.astype(F32)
        keep = jnp.where(in_grp[g], (rank < TOPK_GROUPS).astype(F32), keep)
    cur = jnp.where(keep > 0.0, sel, -jnp.inf)
    ids, top_s = [], []
    onehot = jnp.zeros((rows, e), F32)
    for _ in range(TOP_K):
        _, ik = _first_argmax(cur, lane_f, e)
        hit = lane_f == ik
        top_s.append(jnp.sum(jnp.where(hit, scores, 0.0), axis=-1, keepdims=True))
        cur = jnp.where(hit, -jnp.inf, cur)
        onehot = jnp.where(hit, 1.0, onehot)
        ids.append(ik)
    den = top_s[0]
    for s in top_s[1:]:
        den = den + s
    gates = [s / den * ROUTED_SCALE for s in top_s]
    return ids, gates, onehot


def _pack_lanes(cols, width):
    rows = cols[0].shape[0]
    lane = lax.broadcasted_iota(jnp.int32, (rows, width), 1)
    out = jnp.zeros((rows, width), cols[0].dtype)
    for k, c in enumerate(cols):
        out = jnp.where(lane == k, c, out)
    return out


def _ln_router_kernel(u_ref, g_ref, b_ref, wr_ref, br_ref, h_ref, hb_ref, idx_ref, gate_ref, cnt_ref):
    h = _layer_norm_rows(u_ref[...], g_ref[...], b_ref[...])
    h_ref[...] = h
    hb = _mx(h)
    hb_ref[...] = hb
    logits = jnp.dot(hb, _mx(wr_ref[...]), preferred_element_type=F32)
    ids, gates, onehot = _route_rows(logits, br_ref[...])
    idx_ref[...] = _pack_lanes(ids, idx_ref.shape[1]).astype(jnp.int32)
    gate_ref[...] = _pack_lanes(gates, gate_ref.shape[1])
    cnt_ref[...] = jnp.broadcast_to(jnp.sum(onehot, axis=0, keepdims=True), cnt_ref.shape)


def _ln_router(u, g, b, w_router, b_router):
    t, d = u.shape
    e = w_router.shape[1]
    tr = _tile(t, 256)
    nt = t // tr
    row = pl.BlockSpec((tr, d), lambda i: (i, 0))
    vec = pl.BlockSpec((1, d), lambda i: (0, 0))
    lanes = pl.BlockSpec((tr, 128), lambda i: (i, 0))
    return pl.pallas_call(
        _ln_router_kernel, grid=(nt,),
        in_specs=[row, vec, vec, pl.BlockSpec((d, e), lambda i: (0, 0)), pl.BlockSpec((1, e), lambda i: (0, 0))],
        out_specs=[row, row, lanes, lanes, pl.BlockSpec((None, 8, e), lambda i: (i, 0, 0))],
        out_shape=[jax.ShapeDtypeStruct((t, d), F32), jax.ShapeDtypeStruct((t, d), MXU_DTYPE),
                   jax.ShapeDtypeStruct((t, 128), jnp.int32), jax.ShapeDtypeStruct((t, 128), F32),
                   jax.ShapeDtypeStruct((nt, 8, e), F32)],
        compiler_params=_cparams(("parallel",)), name="ln1_router",
    )(u, g, b, w_router, b_router)


def _dest_kernel(idx_ref, start_ref, dest_ref, carry_ref):
    @pl.when(pl.program_id(0) == 0)
    def _():
        carry_ref[...] = jnp.zeros_like(carry_ref)

    rows = idx_ref.shape[0]
    e = start_ref.shape[1]
    idx = idx_ref[...].astype(F32)
    lane_f = lax.broadcasted_iota(jnp.int32, (rows, e), 1).astype(F32)
    hits = [lane_f == idx[:, k:k + 1] for k in range(TOP_K)]
    onehot = jnp.zeros((rows, e), F32)
    for hit in hits:
        onehot = jnp.where(hit, 1.0, onehot)
    r = lax.broadcasted_iota(jnp.int32, (rows, rows), 0)
    c = lax.broadcasted_iota(jnp.int32, (rows, rows), 1)
    earlier = jnp.where(c < r, 1.0, 0.0)
    base = _dot(earlier, onehot) + carry_ref[0:1, :] + start_ref[...]
    dest = [jnp.sum(jnp.where(hit, base, 0.0), axis=-1, keepdims=True) for hit in hits]
    dest_ref[...] = _pack_lanes(dest, dest_ref.shape[1]).astype(jnp.int32)
    carry_ref[...] = carry_ref[...] + jnp.sum(onehot, axis=0, keepdims=True)


def _dest_rows(idx, start):
    t = idx.shape[0]
    e = start.shape[1]
    tr = _tile(t, 256)
    return pl.pallas_call(
        _dest_kernel, grid=(t // tr,),
        in_specs=[pl.BlockSpec((tr, 128), lambda i: (i, 0)), pl.BlockSpec((1, e), lambda i: (0, 0))],
        out_specs=pl.BlockSpec((tr, 128), lambda i: (i, 0)),
        out_shape=jax.ShapeDtypeStruct((t, 128), jnp.int32),
        scratch_shapes=[pltpu.VMEM((8, e), F32)],
        compiler_params=_cparams(("arbitrary",)), name="expert_rows",
    )(idx, start)


def _ln_kernel(u_ref, g_ref, b_ref, x_ref, xb_ref):
    x2 = _layer_norm_rows(u_ref[...], g_ref[...], b_ref[...])
    x_ref[...] = x2
    xb_ref[...] = _mx(x2)


def _ln(u, g, b):
    t, d = u.shape
    tr = _tile(t, 256)
    row = pl.BlockSpec((tr, d), lambda i: (i, 0))
    vec = pl.BlockSpec((1, d), lambda i: (0, 0))
    return pl.pallas_call(
        _ln_kernel, grid=(t // tr,), in_specs=[row, vec, vec], out_specs=[row, row],
        out_shape=[jax.ShapeDtypeStruct((t, d), F32), jax.ShapeDtypeStruct((t, d), MXU_DTYPE)],
        compiler_params=_cparams(("parallel",)), name="ln2",
    )(u, g, b)


def _expert_kernel(be_ref, nu_ref, x_ref, wg_ref, wu_ref, wd_ref, y_ref, acc_ref):
    blk = pl.program_id(0)
    f = pl.program_id(1)

    @pl.when(blk < nu_ref[0])
    def _():
        x = x_ref[...]
        g = jnp.dot(x, _mx(wg_ref[...]), preferred_element_type=F32)
        u = jnp.dot(x, _mx(wu_ref[...]), preferred_element_type=F32)
        part = _dot(jax.nn.silu(g) * u, wd_ref[...])

        @pl.when(f == 0)
        def _():
            acc_ref[...] = part

        @pl.when(f > 0)
        def _():
            acc_ref[...] += part

        @pl.when(f == pl.num_programs(1) - 1)
        def _():
            y_ref[...] = acc_ref[...].astype(y_ref.dtype)


def _grouped_experts(x_sorted, block_expert, n_used, w_gate, w_up, w_down):
    p, d = x_sorted.shape
    n_blocks = p // EXPERT_ROWS
    dff = w_gate.shape[-1]
    tf = _tile(dff, EXPERT_FF_TILE)
    nf = dff // tf

    def live(b, nu):
        return jnp.minimum(b, nu[0] - 1)

    def ff(b, f, nu):
        return jnp.where(b < nu[0], f, nf - 1)

    grid_spec = pltpu.PrefetchScalarGridSpec(
        num_scalar_prefetch=2, grid=(n_blocks, nf),
        in_specs=[
            pl.BlockSpec((EXPERT_ROWS, d), lambda b, f, be, nu: (live(b, nu), 0)),
            pl.BlockSpec((None, d, tf), lambda b, f, be, nu: (be[live(b, nu)], 0, ff(b, f, nu))),
            pl.BlockSpec((None, d, tf), lambda b, f, be, nu: (be[live(b, nu)], 0, ff(b, f, nu))),
            pl.BlockSpec((None, tf, d), lambda b, f, be, nu: (be[live(b, nu)], ff(b, f, nu), 0)),
        ],
        out_specs=pl.BlockSpec((EXPERT_ROWS, d), lambda b, f, be, nu: (live(b, nu), 0)),
        scratch_shapes=[pltpu.VMEM((EXPERT_ROWS, d), F32)],
    )
    return pl.pallas_call(
        _expert_kernel, grid_spec=grid_spec, out_shape=jax.ShapeDtypeStruct((p, d), MXU_DTYPE),
        compiler_params=_cparams(("arbitrary", "arbitrary")), name="grouped_experts",
    )(block_expert, n_used, x_sorted, w_gate, w_up, w_down)


def _routed_experts(hb, idx, counts, w_gate, w_up, w_down):
    t, d = hb.shape
    e = w_gate.shape[0]
    a = t * TOP_K
    counts = counts.astype(jnp.int32)
    blocks_per = (counts + EXPERT_ROWS - 1) // EXPERT_ROWS
    blk_end = jnp.cumsum(blocks_per)
    start = (blk_end - blocks_per) * EXPERT_ROWS
    n_blocks = -(-a // EXPERT_ROWS) + e
    p = n_blocks * EXPERT_ROWS
    block_expert = jnp.minimum(
        jnp.searchsorted(blk_end, jnp.arange(n_blocks, dtype=jnp.int32), side="right"), e - 1).astype(jnp.int32)
    n_used = blk_end[-1:].astype(jnp.int32)
    dest = _dest_rows(idx, start.astype(F32)[None, :])[:, :TOP_K].reshape(a)
    slot_tok = (jnp.arange(p, dtype=jnp.int32) % t).at[dest].set(jnp.arange(a, dtype=jnp.int32) // TOP_K)
    x_sorted = hb[slot_tok]
    y = _grouped_experts(x_sorted, block_expert, n_used, w_gate, w_up, w_down)
    return y[dest].reshape(t, TOP_K * d)


def kernel(x_prompt, x_sample, state_ret, state_mlstm_c, state_mlstm_n, state_mlstm_m, state_conv, p_prompt, p_sample, w_in, b_gate_i, b_gate_f, conv_w, conv_b, g_ret, g_mlstm, w_br_r, w_br_m, w_out, ln1_g, ln1_b, w_router, b_router, w_e_gate, w_e_up, w_e_down, w_s_gate, w_s_up, w_s_down, ln2_g, ln2_b, w_pe, w_pg):
    depth = w_in.shape[0]
    assert depth == 1, "single-layer step"
    alpha = (2 * depth) ** 0.25
    bp, lp, d = x_prompt.shape
    bs, ls, _ = x_sample.shape
    assert bp == 1
    hd = state_ret.shape[-1]
    w = N_HEADS * hd
    conv_hist = state_conv.shape[2]
    off = {"rq": 0, "rk": w, "rv": 2 * w, "rg": 3 * w, "mq": 4 * w, "mk": 5 * w, "mv": 6 * w, "mo": 7 * w}
    off_mi = 8 * w
    off_ga = off_mi + 2 * N_HEADS
    n_main = 8 * w
    tp, ts = bp * lp, bs * ls
    t = tp + ts

    x = jnp.concatenate([x_prompt.reshape(tp, d), x_sample.reshape(ts, d)], axis=0)
    xb = _mx(x)
    w_in0 = w_in[0]
    first = lambda accs, extras: accs[0]

    z_main = _fused_matmul([xb], [(w_in0, 0, 0)], [], first, [F32], n_main, name="proj_main")
    w_if = jnp.pad(w_in0[:, off_mi:off_ga], ((0, 0), (0, 128 - 2 * N_HEADS)))
    z_if = _fused_matmul([xb], [(w_if, 0, 0)], [], first, [F32], 128, name="proj_gates")
    w_gab = w_in0[:, off_ga:]
    z_gab = _fused_matmul([xb], [(w_gab, 0, 0)], [], first, [F32], 2 * d, name="proj_merge_gates")

    u = z_main[:, off["mq"]:off["mv"]]
    u_p = jnp.concatenate([jnp.zeros((bp, conv_hist, 2 * w), F32), u[:tp].reshape(bp, lp, 2 * w)], axis=1)
    u_s = jnp.concatenate([state_conv[0], u[tp:].reshape(bs, ls, 2 * w)], axis=1)

    def conv(u_ext, length):
        acc = conv_b[0]
        for j in range(conv_hist + 1):
            acc = acc + conv_w[0, j] * u_ext[:, j:j + length]
        return jax.nn.silu(acc)

    qk_m = jnp.concatenate([conv(u_p, lp).reshape(tp, 2 * w), conv(u_s, ls).reshape(ts, 2 * w)], axis=0)
    conv_p = u_p[:, -conv_hist:][None]
    conv_s = u_s[:, -conv_hist:][None]

    pos = jnp.concatenate([jnp.tile(jnp.arange(lp, dtype=jnp.int32), bp),
                           jnp.tile(PAST_LEN + jnp.arange(ls, dtype=jnp.int32), bs)])
    half = hd // 2
    inv_freq = ROPE_BASE ** (-jnp.arange(half, dtype=F32) / half)
    ang = pos.astype(F32)[:, None] * inv_freq[None, :]
    cos, sin = jnp.cos(ang), jnp.sin(ang)

    mi = z_if[:, :N_HEADS] + b_gate_i[0]
    lf = jax.nn.log_sigmoid(z_if[:, N_HEADS:2 * N_HEADS] + b_gate_f[0])
    cl_p = _tile(lp, PROMPT_CHUNK)
    cl_s = ls

    def gate_layouts(rows, cl):
        b = jnp.cumsum(lf[rows].reshape(-1, cl, N_HEADS), axis=1)
        i = mi[rows].reshape(-1, cl, N_HEADS)
        col = jnp.concatenate([b, i], axis=-1).reshape(-1, 2 * N_HEADS)
        row = jnp.stack([jnp.swapaxes(b, 1, 2), jnp.swapaxes(i, 1, 2)], axis=2)
        return jnp.pad(col, ((0, 0), (0, 128 - 2 * N_HEADS))), jnp.pad(row, ((0, 0), (0, 0), (0, 6), (0, 0)))

    gcol_p, grow_p = gate_layouts(slice(0, tp), cl_p)
    gcol_s, grow_s = gate_layouts(slice(tp, t), cl_s)
    gcol = jnp.concatenate([gcol_p, gcol_s], axis=0)

    zeros = lambda *shape: jnp.zeros(shape, F32)
    rec = functools.partial(_recurrences, z_main, qk_m, cos, sin, gcol, g_ret=g_ret, g_mlstm=g_mlstm, offs=off)
    yr_p, ym_p, s_p, c_p, n_p, m_p = rec(
        grow_p, s0=zeros(bp, N_HEADS, hd, hd), c0=zeros(bp, N_HEADS, hd, hd),
        n0=zeros(bp, N_HEADS, 1, hd), m0=zeros(bp, N_HEADS, 1, 128), row0=0, n_seq=bp, seq_len=lp, cl=cl_p,
        name="recurrence_prompt")
    m0_s = jnp.broadcast_to(state_mlstm_m[0][:, :, None, None], (bs, N_HEADS, 1, 128))
    yr_s, ym_s, s_s, c_s, n_s, m_s = rec(
        grow_s, s0=state_ret[0], c0=state_mlstm_c[0], n0=state_mlstm_n[0][:, :, None, :], m0=m0_s,
        row0=tp, n_seq=bs, seq_len=ls, cl=cl_s, name="recurrence_sample")
    ret_out = jnp.concatenate([yr_p, yr_s], axis=0)
    mls_out = jnp.concatenate([ym_p, ym_s], axis=0)

    def merge(accs, extras):
        return jax.nn.sigmoid(extras[0]) * accs[0] + jax.nn.sigmoid(extras[1]) * accs[1]

    merged = _fused_matmul([ret_out, mls_out], [(w_br_r[0], 0, 0), (w_br_m[0], 0, 1)], [(z_gab, 0), (z_gab, d)],
                           merge, [MXU_DTYPE], d, tm=512, name="merge")

    u1 = _fused_matmul([merged], [(w_out[0], 0, 0)], [(x, 0)], lambda accs, extras: alpha * extras[0] + accs[0],
                       [F32], d, name="out_proj")
    h, hb, idx, gates, cnt = _ln_router(u1, ln1_g, ln1_b, w_router[0], b_router)

    yk = _routed_experts(hb, idx, cnt[:, 0, :].sum(0), w_e_gate[0], w_e_up[0], w_e_down[0])
    act = _fused_matmul([hb], [(w_s_gate[0], 0, 0), (w_s_up[0], 0, 0)], [],
                        lambda accs, extras: jax.nn.silu(accs[0]) * accs[1], [MXU_DTYPE], w_s_gate.shape[-1],
                        tm=512, tn=256, name="shared_up")

    def ffn_residual(accs, extras):
        routed = extras[2].astype(F32) * extras[1][:, 0:1]
        for k in range(1, TOP_K):
            routed = routed + extras[2 + k].astype(F32) * extras[1][:, k:k + 1]
        return alpha * extras[0] + (routed + accs[0])

    u2 = _fused_matmul([act], [(w_s_down[0], 0, 0)], [(h, 0), (gates, None)] + [(yk, k * d) for k in range(TOP_K)],
                       ffn_residual, [F32], d, tm=512, name="shared_down")

    x2, x2b = _ln(u2, ln2_g, ln2_b)
    p = _mx(jnp.concatenate([p_prompt[0].reshape(tp, -1), p_sample[0].reshape(ts, -1)], axis=0))
    out = _fused_matmul([x2b, p], [(w_pg[0], 0, 0), (w_pe[0], 0, 1)], [(x2, 0)],
                        lambda accs, extras: extras[0] + jax.nn.sigmoid(accs[0]) * accs[1], [F32], d, tm=512,
                        name="embed_gate")

    dt = x_prompt.dtype
    y_p = out[:tp].reshape(bp, lp, d).astype(dt)
    y_s = out[tp:].reshape(bs, ls, d).astype(dt)
    return (y_p, y_s,
            s_p[None].astype(dt), c_p[None].astype(dt), n_p[:, :, 0][None].astype(dt), m_p[:, :, 0, 0][None].astype(dt),
            conv_p.astype(dt),
            s_s[None].astype(dt), c_s[None].astype(dt), n_s[:, :, 0][None].astype(dt), m_s[:, :, 0, 0][None].astype(dt),
            conv_s.astype(dt))
```

```python
import functools

import jax
import jax.numpy as jnp
from jax import lax
from jax.experimental import pallas as pl
from jax.experimental.pallas import tpu as pltpu

F32 = jnp.float32
MXU_DTYPE = jnp.bfloat16

N_HEADS = 8
PAST_LEN = 1024
TOP_K = 8
N_GROUPS = 8
TOPK_GROUPS = 4
ROUTED_SCALE = 2.5
ROPE_BASE = 10000.0
LN_EPS = 1e-5

V7X_VMEM_LIMIT_BYTES = 56 * 1024 * 1024
PROMPT_CHUNK = 256
EXPERT_ROWS = 384
EXPERT_FF_TILE = 256
X_ROW_ALIGN = 16


def _cparams(sem):
    return pltpu.CompilerParams(dimension_semantics=sem, vmem_limit_bytes=V7X_VMEM_LIMIT_BYTES)


def _tile(n, pref):
    if n <= pref:
        return n
    t = pref
    while n % t:
        t //= 2
    return t


def _mx(a):
    return a.astype(MXU_DTYPE)


def _dot(a, b):
    return jnp.dot(_mx(a), _mx(b), preferred_element_type=F32)


def _fused_matmul_kernel(*refs, n_lhs, lhs_of, n_extra, epilogue):
    n_rhs = len(lhs_of)
    lhs = refs[:n_lhs]
    rhs = refs[n_lhs:n_lhs + n_rhs]
    extra = refs[n_lhs + n_rhs:n_lhs + n_rhs + n_extra]
    outs = refs[n_lhs + n_rhs + n_extra:]
    accs = [_dot(lhs[j][...], b[...]) for j, b in zip(lhs_of, rhs)]
    res = epilogue(accs, [e[...] for e in extra])
    if not isinstance(res, (tuple, list)):
        res = (res,)
    for o, r in zip(outs, res):
        o[...] = r.astype(o.dtype)


def _fused_matmul(lhs, rhs, extras, epilogue, out_dtypes, n_cols, *, tm=1024, tn=512, name):
    m = lhs[0].shape[0]
    tm = _tile(m, tm)
    tn = _tile(n_cols, tn)
    in_specs = []
    for a in lhs:
        in_specs.append(pl.BlockSpec((tm, a.shape[1]), lambda n, i: (i, 0)))
    for (b, off, _) in rhs:
        assert off % tn == 0
        in_specs.append(pl.BlockSpec((b.shape[0], tn), functools.partial(lambda n, i, o: (0, n + o), o=off // tn)))
    for (e, off, *row_off) in extras:
        if off is None:
            in_specs.append(pl.BlockSpec((tm, e.shape[1]), lambda n, i: (i, 0)))
        else:
            r0 = row_off[0] if row_off else 0
            assert off % tn == 0 and r0 % tm == 0
            in_specs.append(pl.BlockSpec(
                (tm, tn), functools.partial(lambda n, i, o, r: (i + r, n + o), o=off // tn, r=r0 // tm)))
    out_shape = [jax.ShapeDtypeStruct((m, n_cols), dt) for dt in out_dtypes]
    out_specs = [pl.BlockSpec((tm, tn), lambda n, i: (i, n)) for _ in out_dtypes]
    kern = functools.partial(_fused_matmul_kernel, n_lhs=len(lhs), lhs_of=tuple(j for _, _, j in rhs),
                             n_extra=len(extras), epilogue=epilogue)
    res = pl.pallas_call(
        kern, grid=(n_cols // tn, m // tm), in_specs=in_specs, out_specs=out_specs, out_shape=out_shape,
        compiler_params=_cparams(("parallel", "arbitrary")), name=name,
    )(*lhs, *[b for b, _, _ in rhs], *[e[0] for e in extras])
    return res[0] if len(res) == 1 else res


def _rotary(t, cos, sin):
    half = t.shape[-1] // 2
    t1, t2 = t[:, :half], t[:, half:]
    return jnp.concatenate([t1 * cos - t2 * sin, t1 * sin + t2 * cos], axis=-1)


def _head_norm(y, g):
    mu = jnp.mean(y, axis=-1, keepdims=True)
    d = y - mu
    var = jnp.mean(d * d, axis=-1, keepdims=True)
    return d * lax.rsqrt(var + LN_EPS) * g


def _dot_nt(a, b):
    return lax.dot_general(_mx(a), _mx(b), (((1,), (1,)), ((), ())), preferred_element_type=F32)


def _dot_tn(a, b):
    return lax.dot_general(_mx(a), _mx(b), (((0,), (0,)), ((), ())), preferred_element_type=F32)


def _causal_conv_silu(u, hist_ref, w_ref, b_ref):
    cl = u.shape[0]
    n_hist = hist_ref.shape[0]
    taps = w_ref.shape[0]
    ext = jnp.concatenate([hist_ref[...], u], axis=0)
    acc = b_ref[...] + w_ref[taps - 1:taps, :] * u
    for s in range(1, taps):
        acc = acc + w_ref[taps - 1 - s:taps - s, :] * pltpu.roll(ext, s, axis=0)[n_hist:n_hist + cl]
    hist_ref[...] = u[cl - n_hist:cl, :]
    return jax.nn.silu(acc)


def _recurrence_kernel(rq_ref, rk_ref, rv_ref, rg_ref, mq_ref, mk_ref, mv_ref, mo_ref, cos_ref, sin_ref,
                       gcol_ref, grow_ref, dec_ref, qd_ref, kd_ref, sl_ref, gr_ref, gm_ref,
                       cwq_ref, cwk_ref, cbq_ref, cbk_ref, hq_in, hk_in,
                       s_in, c_in, n_in, m_in,
                       yr_ref, ym_ref, s_ref, c_ref, n_ref, m_ref, hq_ref, hk_ref, *, qk_scale):
    head = pl.program_id(1)
    chunk = pl.program_id(2)
    cl = rq_ref.shape[0]

    @pl.when(chunk == 0)
    def _():
        s_ref[...] = s_in[...]
        c_ref[...] = c_in[...]
        n_ref[...] = n_in[...]
        m_ref[...] = m_in[...]
        n_hist = hq_in.shape[0]
        for h_ref, h_in in ((hq_ref, hq_in), (hk_ref, hk_in)):
            h_ref[...] = jnp.zeros_like(h_ref)
            h_ref[h_ref.shape[0] - n_hist:, :] = h_in[...]

    cos = cos_ref[...]
    sin = sin_ref[...]

    q = _rotary(rq_ref[...], cos, sin)
    k = _rotary(rk_ref[...], cos, sin) * qk_scale
    v = rv_ref[...]
    s_old = s_ref[...]
    scores = _dot_nt(q, k) * dec_ref[...]
    yr = _dot(scores, v) + _dot(q, s_old) * qd_ref[...]
    s_ref[...] = sl_ref[...] * s_old + _dot_tn(k * kd_ref[...], v)
    yr_ref[...] = (_head_norm(yr, gr_ref[...]) * jax.nn.silu(rg_ref[...])).astype(yr_ref.dtype)

    lane = lax.broadcasted_iota(jnp.int32, gcol_ref.shape, 1)
    gcol = gcol_ref[...]
    b_col = jnp.sum(jnp.where(lane == head, gcol, 0.0), axis=-1, keepdims=True)
    i_col = jnp.sum(jnp.where(lane == head + N_HEADS, gcol, 0.0), axis=-1, keepdims=True)
    b_row = grow_ref[0:1, :]
    i_row = grow_ref[1:2, :]
    m_prev = m_ref[0:1, 0:1]
    t_idx = lax.broadcasted_iota(jnp.int32, (cl, cl), 0)
    s_idx = lax.broadcasted_iota(jnp.int32, (cl, cl), 1)
    dmat = jnp.where(t_idx >= s_idx, b_col - b_row + i_row, -jnp.inf)
    a_col = b_col + m_prev
    m_t = jnp.maximum(a_col, jnp.max(dmat, axis=-1, keepdims=True))
    w_inter = jnp.exp(a_col - m_t)
    w_intra = jnp.exp(dmat - m_t)
    mq = _causal_conv_silu(mq_ref[...], hq_ref, cwq_ref, cbq_ref)
    mk = _causal_conv_silu(mk_ref[...], hk_ref, cwk_ref, cbk_ref) * qk_scale
    mv = mv_ref[...]
    c_old = c_ref[...]
    n_old = n_ref[...]
    qk = _dot_nt(mq, mk) * w_intra
    num = _dot(qk, mv) + w_inter * _dot(mq, c_old)
    den = jnp.sum(qk, axis=-1, keepdims=True) + w_inter * jnp.sum(mq * n_old, axis=-1, keepdims=True)
    ym = num / jnp.maximum(jnp.abs(den), jnp.exp(-m_t))
    m_last = m_t[cl - 1:cl, :]
    wl_inter = w_inter[cl - 1:cl, :]
    wl_col = jnp.exp(b_col[cl - 1:cl, :] - b_col + i_col - m_last)
    c_ref[...] = wl_inter * c_old + _dot_tn(mk * wl_col, mv)
    n_ref[...] = wl_inter * n_old + jnp.sum(mk * wl_col, axis=0, keepdims=True)
    m_ref[...] = jnp.broadcast_to(m_last, m_ref.shape)
    ym_ref[...] = (_head_norm(ym, gm_ref[...]) * jax.nn.sigmoid(mo_ref[...])).astype(ym_ref.dtype)


def _recurrences(z_main, cos, sin, gcol, conv_w, conv_b, grow, conv0, s0, c0, n0, m0, *, g_ret, g_mlstm, row0, n_seq,
                 seq_len, cl, offs, name):
    hd = s0.shape[-1]
    w = N_HEADS * hd
    taps = conv_w.shape[0]
    n_chunks = seq_len // cl
    rb0 = row0 // cl
    cb = {k: v // hd for k, v in offs.items()}

    def rows(b, c):
        return rb0 + b * n_chunks + c

    def zspec(col_key):
        return pl.BlockSpec((cl, hd), functools.partial(lambda b, h, c, o: (rows(b, c), h + o), o=cb[col_key]))

    log_g = jnp.log1p(-jnp.exp2(-5.0 - jnp.arange(N_HEADS, dtype=F32)))
    t = jnp.arange(cl, dtype=F32)
    diff = t[:, None] - t[None, :]
    dec = jnp.exp(jnp.where((diff >= 0)[None], diff[None] * log_g[:, None, None], -jnp.inf))
    qd = jnp.broadcast_to(jnp.exp((t[None, :] + 1.0) * log_g[:, None])[:, :, None], (N_HEADS, cl, hd))
    kd = jnp.broadcast_to(jnp.exp((cl - 1.0 - t)[None, :] * log_g[:, None])[:, :, None], (N_HEADS, cl, hd))
    sl = jnp.broadcast_to(jnp.exp(cl * log_g)[:, None, None], (N_HEADS, 1, hd))

    head_const = lambda shape: pl.BlockSpec((None,) + shape, lambda b, h, c: (h, 0, 0))
    state = lambda shape: pl.BlockSpec((None, None) + shape, lambda b, h, c: (b, h, 0, 0))
    in_specs = [
        zspec("rq"), zspec("rk"), zspec("rv"), zspec("rg"), zspec("mq"), zspec("mk"), zspec("mv"), zspec("mo"),
        pl.BlockSpec((cl, hd // 2), lambda b, h, c: (rows(b, c), 0)),
        pl.BlockSpec((cl, hd // 2), lambda b, h, c: (rows(b, c), 0)),
        pl.BlockSpec((cl, 128), lambda b, h, c: (rows(b, c), 0)),
        pl.BlockSpec((None, None, 8, cl), lambda b, h, c: (b * n_chunks + c, h, 0, 0)),
        head_const((cl, cl)), head_const((cl, hd)), head_const((cl, hd)), head_const((1, hd)),
        pl.BlockSpec((1, hd), lambda b, h, c: (0, h)), pl.BlockSpec((1, hd), lambda b, h, c: (0, h)),
        pl.BlockSpec((taps, hd), lambda b, h, c: (0, h)), pl.BlockSpec((taps, hd), lambda b, h, c: (0, h + N_HEADS)),
        pl.BlockSpec((1, hd), lambda b, h, c: (0, h)), pl.BlockSpec((1, hd), lambda b, h, c: (0, h + N_HEADS)),
        pl.BlockSpec((None, taps - 1, hd), lambda b, h, c: (b, 0, h)),
        pl.BlockSpec((None, taps - 1, hd), lambda b, h, c: (b, 0, h + N_HEADS)),
        state((hd, hd)), state((hd, hd)), state((1, hd)), state((1, 128)),
    ]
    n_rows = n_seq * seq_len
    out_shape = [
        jax.ShapeDtypeStruct((n_rows, w), MXU_DTYPE), jax.ShapeDtypeStruct((n_rows, w), MXU_DTYPE),
        jax.ShapeDtypeStruct((n_seq, N_HEADS, hd, hd), F32), jax.ShapeDtypeStruct((n_seq, N_HEADS, hd, hd), F32),
        jax.ShapeDtypeStruct((n_seq, N_HEADS, 1, hd), F32), jax.ShapeDtypeStruct((n_seq, N_HEADS, 1, 128), F32),
    ]
    y_spec = pl.BlockSpec((cl, hd), lambda b, h, c: (b * n_chunks + c, h))
    out_specs = [y_spec, y_spec, state((hd, hd)), state((hd, hd)), state((1, hd)), state((1, 128))]
    kern = functools.partial(_recurrence_kernel, qk_scale=float(hd) ** -0.5)
    hist_rows = 8
    assert taps - 1 <= hist_rows <= cl
    return pl.pallas_call(
        kern, grid=(n_seq, N_HEADS, n_chunks), in_specs=in_specs, out_specs=out_specs, out_shape=out_shape,
        scratch_shapes=[pltpu.VMEM((hist_rows, hd), F32), pltpu.VMEM((hist_rows, hd), F32)],
        compiler_params=_cparams(("parallel", "parallel", "arbitrary")), name=name,
    )(z_main, z_main, z_main, z_main, z_main, z_main, z_main, z_main, cos, sin, gcol, grow, dec, qd, kd, sl,
      g_ret, g_mlstm, conv_w, conv_w, conv_b, conv_b, conv0, conv0, s0, c0, n0, m0)


def _layer_norm_rows(u, g, b):
    mu = jnp.mean(u, axis=-1, keepdims=True)
    d = u - mu
    var = jnp.mean(d * d, axis=-1, keepdims=True)
    return d * lax.rsqrt(var + LN_EPS) * g + b


def _first_argmax(vals, lane_f, n):
    mx = jnp.max(vals, axis=-1, keepdims=True)
    return mx, jnp.min(jnp.where(vals == mx, lane_f, float(n)), axis=-1, keepdims=True)


def _route_rows(logits, bias):
    rows, e = logits.shape
    gsz = e // N_GROUPS
    lane = lax.broadcasted_iota(jnp.int32, (rows, e), 1)
    lane_f = lane.astype(F32)
    scores = jax.nn.sigmoid(logits)
    sel = scores + bias
    in_grp = [(lane >= g * gsz) & (lane < (g + 1) * gsz) for g in range(N_GROUPS)]
    grp_score = []
    for g in range(N_GROUPS):
        m = jnp.where(in_grp[g], sel, -jnp.inf)
        m1, i1 = _first_argmax(m, lane_f, e)
        m2 = jnp.max(jnp.where(lane_f == i1, -jnp.inf, m), axis=-1, keepdims=True)
        grp_score.append(m1 + m2)
    keep = jnp.zeros((rows, e), F32)
    for g in range(N_GROUPS):
        rank = jnp.zeros_like(grp_score[g])
        for o in range(N_GROUPS):
            if o != g:
                ahead = (grp_score[o] >= grp_score[g]) if o < g else (grp_score[o] > grp_score[g])
                rank = rank + ahead.astype(F32)
        keep = jnp.where(in_grp[g], (rank < TOPK_GROUPS).astype(F32), keep)
    cur = jnp.where(keep > 0.0, sel, -jnp.inf)
    ids, top_s = [], []
    onehot = jnp.zeros((rows, e), F32)
    for _ in range(TOP_K):
        _, ik = _first_argmax(cur, lane_f, e)
        hit = lane_f == ik
        top_s.append(jnp.sum(jnp.where(hit, scores, 0.0), axis=-1, keepdims=True))
        cur = jnp.where(hit, -jnp.inf, cur)
        onehot = jnp.where(hit, 1.0, onehot)
        ids.append(ik)
    den = top_s[0]
    for s in top_s[1:]:
        den = den + s
    gates = [s / den * ROUTED_SCALE for s in top_s]
    return ids, gates, onehot


def _pack_lanes(cols, width):
    rows = cols[0].shape[0]
    lane = lax.broadcasted_iota(jnp.int32, (rows, width), 1)
    out = jnp.zeros((rows, width), cols[0].dtype)
    for k, c in enumerate(cols):
        out = jnp.where(lane == k, c, out)
    return out


def _ln_router_kernel(u_ref, g_ref, b_ref, wr_ref, br_ref, h_ref, hb_ref, idx_ref, gate_ref, cnt_ref):
    h = _layer_norm_rows(u_ref[...], g_ref[...], b_ref[...])
    h_ref[...] = h
    hb = _mx(h)
    hb_ref[...] = hb
    logits = jnp.dot(hb, _mx(wr_ref[...]), preferred_element_type=F32)
    ids, gates, onehot = _route_rows(logits, br_ref[...])
    idx_ref[...] = _pack_lanes(ids, idx_ref.shape[1]).astype(jnp.int32)
    gate_ref[...] = _pack_lanes(gates, gate_ref.shape[1])
    cnt_ref[...] = jnp.broadcast_to(jnp.sum(onehot, axis=0, keepdims=True), cnt_ref.shape)


def _ln_router(u, g, b, w_router, b_router):
    t, d = u.shape
    e = w_router.shape[1]
    tr = _tile(t, 256)
    nt = t // tr
    row = pl.BlockSpec((tr, d), lambda i: (i, 0))
    vec = pl.BlockSpec((1, d), lambda i: (0, 0))
    lanes = pl.BlockSpec((tr, 128), lambda i: (i, 0))
    return pl.pallas_call(
        _ln_router_kernel, grid=(nt,),
        in_specs=[row, vec, vec, pl.BlockSpec((d, e), lambda i: (0, 0)), pl.BlockSpec((1, e), lambda i: (0, 0))],
        out_specs=[row, row, lanes, lanes, pl.BlockSpec((None, 8, e), lambda i: (i, 0, 0))],
        out_shape=[jax.ShapeDtypeStruct((t, d), F32), jax.ShapeDtypeStruct((t, d), MXU_DTYPE),
                   jax.ShapeDtypeStruct((t, 128), jnp.int32), jax.ShapeDtypeStruct((t, 128), F32),
                   jax.ShapeDtypeStruct((nt, 8, e), F32)],
        compiler_params=_cparams(("parallel",)), name="ln1_router",
    )(u, g, b, w_router, b_router)


def _dest_kernel(idx_ref, start_ref, *refs):
    dest_refs, carry_ref = refs[:-1], refs[-1]

    @pl.when(pl.program_id(0) == 0)
    def _():
        carry_ref[...] = jnp.zeros_like(carry_ref)

    rows = idx_ref.shape[0]
    e = start_ref.shape[1]
    idx = idx_ref[...].astype(F32)
    lane_f = lax.broadcasted_iota(jnp.int32, (rows, e), 1).astype(F32)
    hits = [lane_f == idx[:, k:k + 1] for k in range(TOP_K)]
    onehot = jnp.zeros((rows, e), F32)
    for hit in hits:
        onehot = jnp.where(hit, 1.0, onehot)
    r = lax.broadcasted_iota(jnp.int32, (rows, rows), 0)
    c = lax.broadcasted_iota(jnp.int32, (rows, rows), 1)
    earlier = jnp.where(c < r, 1.0, 0.0)
    rank = _dot(earlier, onehot) + carry_ref[0:1, :]
    for j, dest_ref in enumerate(dest_refs):
        base = rank + start_ref[j:j + 1, :]
        dest = [jnp.sum(jnp.where(hit, base, 0.0), axis=-1, keepdims=True) for hit in hits]
        dest_ref[...] = _pack_lanes(dest, dest_ref.shape[1]).astype(jnp.int32)
    carry_ref[...] = carry_ref[...] + jnp.sum(onehot, axis=0, keepdims=True)


def _dest_rows(idx, starts):
    t = idx.shape[0]
    n_layouts, e = starts.shape
    starts = jnp.pad(starts.astype(F32), ((0, 8 - n_layouts), (0, 0)))
    tr = _tile(t, 256)
    lanes = pl.BlockSpec((tr, 128), lambda i: (i, 0))
    return pl.pallas_call(
        _dest_kernel, grid=(t // tr,),
        in_specs=[lanes, pl.BlockSpec((8, e), lambda i: (0, 0))],
        out_specs=[lanes] * n_layouts,
        out_shape=[jax.ShapeDtypeStruct((t, 128), jnp.int32)] * n_layouts,
        scratch_shapes=[pltpu.VMEM((8, e), F32)],
        compiler_params=_cparams(("arbitrary",)), name="expert_rows",
    )(idx, starts)


def _ln_kernel(u_ref, g_ref, b_ref, x_ref, xb_ref):
    x2 = _layer_norm_rows(u_ref[...], g_ref[...], b_ref[...])
    x_ref[...] = x2
    xb_ref[...] = _mx(x2)


def _ln(u, g, b):
    t, d = u.shape
    tr = _tile(t, 256)
    row = pl.BlockSpec((tr, d), lambda i: (i, 0))
    vec = pl.BlockSpec((1, d), lambda i: (0, 0))
    return pl.pallas_call(
        _ln_kernel, grid=(t // tr,), in_specs=[row, vec, vec], out_specs=[row, row],
        out_shape=[jax.ShapeDtypeStruct((t, d), F32), jax.ShapeDtypeStruct((t, d), MXU_DTYPE)],
        compiler_params=_cparams(("parallel",)), name="ln2",
    )(u, g, b)


def _expert_kernel(be_ref, bx_ref, nu_ref, x_ref, wg_ref, wu_ref, wd_ref, y_ref, acc_ref):
    blk = pl.program_id(0)
    f = pl.program_id(1)

    @pl.when(blk < nu_ref[0])
    def _():
        x = x_ref[...]
        g = jnp.dot(x, _mx(wg_ref[...]), preferred_element_type=F32)
        u = jnp.dot(x, _mx(wu_ref[...]), preferred_element_type=F32)
        part = _dot(jax.nn.silu(g) * u, wd_ref[...])

        @pl.when(f == 0)
        def _():
            acc_ref[...] = part

        @pl.when(f > 0)
        def _():
            acc_ref[...] += part

        @pl.when(f == pl.num_programs(1) - 1)
        def _():
            y_ref[...] = acc_ref[...].astype(y_ref.dtype)


def _grouped_experts(x_sorted, block_expert, block_xrow, n_used, n_blocks, w_gate, w_up, w_down):
    d = x_sorted.shape[1]
    dff = w_gate.shape[-1]
    tf = _tile(dff, EXPERT_FF_TILE)
    nf = dff // tf

    def live(b, nu):
        return jnp.minimum(b, nu[0] - 1)

    def ff(b, f, nu):
        return jnp.where(b < nu[0], f, nf - 1)

    grid_spec = pltpu.PrefetchScalarGridSpec(
        num_scalar_prefetch=3, grid=(n_blocks, nf),
        in_specs=[
            pl.BlockSpec((pl.Element(EXPERT_ROWS), pl.Element(d)),
                         lambda b, f, be, bx, nu: (pl.multiple_of(bx[live(b, nu)], X_ROW_ALIGN), 0)),
            pl.BlockSpec((None, d, tf), lambda b, f, be, bx, nu: (be[live(b, nu)], 0, ff(b, f, nu))),
            pl.BlockSpec((None, d, tf), lambda b, f, be, bx, nu: (be[live(b, nu)], 0, ff(b, f, nu))),
            pl.BlockSpec((None, tf, d), lambda b, f, be, bx, nu: (be[live(b, nu)], ff(b, f, nu), 0)),
        ],
        out_specs=pl.BlockSpec((EXPERT_ROWS, d), lambda b, f, be, bx, nu: (live(b, nu), 0)),
        scratch_shapes=[pltpu.VMEM((EXPERT_ROWS, d), F32)],
    )
    return pl.pallas_call(
        _expert_kernel, grid_spec=grid_spec,
        out_shape=jax.ShapeDtypeStruct((n_blocks * EXPERT_ROWS, d), MXU_DTYPE),
        compiler_params=_cparams(("arbitrary", "arbitrary")), name="grouped_experts",
    )(block_expert, block_xrow, n_used, x_sorted, w_gate, w_up, w_down)


def _routed_experts(hb, idx, counts, w_gate, w_up, w_down):
    t, d = hb.shape
    e = w_gate.shape[0]
    a = t * TOP_K
    counts = counts.astype(jnp.int32)
    blocks_per = (counts + EXPERT_ROWS - 1) // EXPERT_ROWS
    blk_end = jnp.cumsum(blocks_per)
    blk_first = blk_end - blocks_per
    y_start = blk_first * EXPERT_ROWS
    x_rows = (counts + X_ROW_ALIGN - 1) // X_ROW_ALIGN * X_ROW_ALIGN
    x_start = jnp.cumsum(x_rows) - x_rows
    n_blocks = -(-a // EXPERT_ROWS) + e
    blocks = jnp.arange(n_blocks, dtype=jnp.int32)
    block_expert = jnp.minimum(jnp.searchsorted(blk_end, blocks, side="right"), e - 1).astype(jnp.int32)
    block_xrow = (x_start[block_expert] + (blocks - blk_first[block_expert]) * EXPERT_ROWS).astype(jnp.int32)
    n_used = blk_end[-1:].astype(jnp.int32)
    dest_x, dest_y = _dest_rows(idx, jnp.stack([x_start, y_start]))
    px = a + X_ROW_ALIGN * e + EXPERT_ROWS
    slot_tok = (jnp.arange(px, dtype=jnp.int32) % t).at[dest_x[:, :TOP_K].reshape(a)].set(
        jnp.arange(a, dtype=jnp.int32) // TOP_K)
    x_sorted = hb[slot_tok]
    y = _grouped_experts(x_sorted, block_expert, block_xrow, n_used, n_blocks, w_gate, w_up, w_down)
    return y[dest_y[:, :TOP_K].T.reshape(a)]


def kernel(x_prompt, x_sample, state_ret, state_mlstm_c, state_mlstm_n, state_mlstm_m, state_conv, p_prompt, p_sample, w_in, b_gate_i, b_gate_f, conv_w, conv_b, g_ret, g_mlstm, w_br_r, w_br_m, w_out, ln1_g, ln1_b, w_router, b_router, w_e_gate, w_e_up, w_e_down, w_s_gate, w_s_up, w_s_down, ln2_g, ln2_b, w_pe, w_pg):
    depth = w_in.shape[0]
    assert depth == 1, "single-layer step"
    alpha = (2 * depth) ** 0.25
    bp, lp, d = x_prompt.shape
    bs, ls, _ = x_sample.shape
    assert bp == 1
    hd = state_ret.shape[-1]
    w = N_HEADS * hd
    conv_hist = state_conv.shape[2]
    off = {"rq": 0, "rk": w, "rv": 2 * w, "rg": 3 * w, "mq": 4 * w, "mk": 5 * w, "mv": 6 * w, "mo": 7 * w}
    off_mi = 8 * w
    n_main = 8 * w
    tp, ts = bp * lp, bs * ls
    t = tp + ts

    x = jnp.concatenate([x_prompt.reshape(tp, d), x_sample.reshape(ts, d)], axis=0)
    xb = _mx(x)
    w_in0 = w_in[0]
    first = lambda accs, extras: accs[0]

    z_main = _fused_matmul([xb], [(w_in0, 0, 0)], [], first, [F32], n_main, name="proj_main")
    tn_tail = _tile(d, 512)
    n_tail = -(-(2 * d + 2 * N_HEADS) // tn_tail) * tn_tail
    z_tail = _fused_matmul([xb], [(w_in0, off_mi, 0)], [], first, [F32], n_tail, tn=tn_tail, name="proj_tail")
    z_if = z_tail[:, :2 * N_HEADS]

    def conv_tail(hist, rows, n_seq, length):
        tail = z_main[rows, off["mq"]:off["mv"]].reshape(n_seq, length, 2 * w)[:, -conv_hist:]
        return jnp.concatenate([hist, tail], axis=1)[:, -conv_hist:][None]

    conv0_p = jnp.zeros((bp, conv_hist, 2 * w), F32)
    conv_p = conv_tail(conv0_p, slice(0, tp), bp, lp)
    conv_s = conv_tail(state_conv[0], slice(tp, t), bs, ls)

    pos = jnp.concatenate([jnp.tile(jnp.arange(lp, dtype=jnp.int32), bp),
                           jnp.tile(PAST_LEN + jnp.arange(ls, dtype=jnp.int32), bs)])
    half = hd // 2
    inv_freq = ROPE_BASE ** (-jnp.arange(half, dtype=F32) / half)
    ang = pos.astype(F32)[:, None] * inv_freq[None, :]
    cos, sin = jnp.cos(ang), jnp.sin(ang)

    mi = z_if[:, :N_HEADS] + b_gate_i[0]
    lf = jax.nn.log_sigmoid(z_if[:, N_HEADS:2 * N_HEADS] + b_gate_f[0])
    cl_p = _tile(lp, PROMPT_CHUNK)
    cl_s = ls

    def gate_layouts(rows, cl):
        b = jnp.cumsum(lf[rows].reshape(-1, cl, N_HEADS), axis=1)
        i = mi[rows].reshape(-1, cl, N_HEADS)
        col = jnp.concatenate([b, i], axis=-1).reshape(-1, 2 * N_HEADS)
        row = jnp.stack([jnp.swapaxes(b, 1, 2), jnp.swapaxes(i, 1, 2)], axis=2)
        return jnp.pad(col, ((0, 0), (0, 128 - 2 * N_HEADS))), jnp.pad(row, ((0, 0), (0, 0), (0, 6), (0, 0)))

    gcol_p, grow_p = gate_layouts(slice(0, tp), cl_p)
    gcol_s, grow_s = gate_layouts(slice(tp, t), cl_s)
    gcol = jnp.concatenate([gcol_p, gcol_s], axis=0)

    zeros = lambda *shape: jnp.zeros(shape, F32)
    rec = functools.partial(_recurrences, z_main, cos, sin, gcol, conv_w[0], conv_b, g_ret=g_ret, g_mlstm=g_mlstm,
                            offs=off)
    yr_p, ym_p, s_p, c_p, n_p, m_p = rec(
        grow_p, conv0_p, s0=zeros(bp, N_HEADS, hd, hd), c0=zeros(bp, N_HEADS, hd, hd),
        n0=zeros(bp, N_HEADS, 1, hd), m0=zeros(bp, N_HEADS, 1, 128), row0=0, n_seq=bp, seq_len=lp, cl=cl_p,
        name="recurrence_prompt")
    m0_s = jnp.broadcast_to(state_mlstm_m[0][:, :, None, None], (bs, N_HEADS, 1, 128))
    yr_s, ym_s, s_s, c_s, n_s, m_s = rec(
        grow_s, state_conv[0], s0=state_ret[0], c0=state_mlstm_c[0], n0=state_mlstm_n[0][:, :, None, :], m0=m0_s,
        row0=tp, n_seq=bs, seq_len=ls, cl=cl_s, name="recurrence_sample")
    ret_out = jnp.concatenate([yr_p, yr_s], axis=0)
    mls_out = jnp.concatenate([ym_p, ym_s], axis=0)

    def merge(accs, extras):
        sh = 2 * N_HEADS
        ga = jnp.concatenate([extras[0][:, sh:], extras[1][:, :sh]], axis=1)
        gb = jnp.concatenate([extras[2][:, sh:], extras[3][:, :sh]], axis=1)
        return jax.nn.sigmoid(ga) * accs[0] + jax.nn.sigmoid(gb) * accs[1]

    merged = _fused_matmul([ret_out, mls_out], [(w_br_r[0], 0, 0), (w_br_m[0], 0, 1)],
                           [(z_tail, 0), (z_tail, tn_tail), (z_tail, d), (z_tail, d + tn_tail)],
                           merge, [MXU_DTYPE], d, tm=512, tn=tn_tail, name="merge")

    u1 = _fused_matmul([merged], [(w_out[0], 0, 0)], [(x, 0)], lambda accs, extras: alpha * extras[0] + accs[0],
                       [F32], d, name="out_proj")
    h, hb, idx, gates, cnt = _ln_router(u1, ln1_g, ln1_b, w_router[0], b_router)

    yk = _routed_experts(hb, idx, cnt[:, 0, :].sum(0), w_e_gate[0], w_e_up[0], w_e_down[0])
    act = _fused_matmul([hb], [(w_s_gate[0], 0, 0), (w_s_up[0], 0, 0)], [],
                        lambda accs, extras: jax.nn.silu(accs[0]) * accs[1], [MXU_DTYPE], w_s_gate.shape[-1],
                        tm=512, tn=256, name="shared_up")

    def ffn_residual(accs, extras):
        routed = extras[2].astype(F32) * extras[1][:, 0:1]
        for k in range(1, TOP_K):
            routed = routed + extras[2 + k].astype(F32) * extras[1][:, k:k + 1]
        return alpha * extras[0] + (routed + accs[0])

    u2 = _fused_matmul([act], [(w_s_down[0], 0, 0)], [(h, 0), (gates, None)] + [(yk, 0, k * t) for k in range(TOP_K)],
                       ffn_residual, [F32], d, tm=512, name="shared_down")

    x2, x2b = _ln(u2, ln2_g, ln2_b)
    p = _mx(jnp.concatenate([p_prompt[0].reshape(tp, -1), p_sample[0].reshape(ts, -1)], axis=0))
    out = _fused_matmul([x2b, p], [(w_pg[0], 0, 0), (w_pe[0], 0, 1)], [(x2, 0)],
                        lambda accs, extras: extras[0] + jax.nn.sigmoid(accs[0]) * accs[1], [F32], d, tm=512,
                        name="embed_gate")

    dt = x_prompt.dtype
    y_p = out[:tp].reshape(bp, lp, d).astype(dt)
    y_s = out[tp:].reshape(bs, ls, d).astype(dt)
    return (y_p, y_s,
            s_p[None].astype(dt), c_p[None].astype(dt), n_p[:, :, 0][None].astype(dt), m_p[:, :, 0, 0][None].astype(dt),
            conv_p.astype(dt),
            s_s[None].astype(dt), c_s[None].astype(dt), n_s[:, :, 0][None].astype(dt), m_s[:, :, 0, 0][None].astype(dt),
            conv_s.astype(dt))
```

```python
import functools

import jax
import jax.numpy as jnp
from jax import lax
from jax.experimental import pallas as pl
from jax.experimental.pallas import tpu as pltpu

F32 = jnp.float32
MXU_DTYPE = jnp.bfloat16

N_HEADS = 8
PAST_LEN = 1024
TOP_K = 8
N_GROUPS = 8
TOPK_GROUPS = 4
ROUTED_SCALE = 2.5
ROPE_BASE = 10000.0
LN_EPS = 1e-5

V7X_VMEM_LIMIT_BYTES = 56 * 1024 * 1024
PROMPT_CHUNK = 256
EXPERT_ROWS = 384
EXPERT_FF_TILE = 256
X_ROW_ALIGN = 16


def _cparams(sem):
    return pltpu.CompilerParams(dimension_semantics=sem, vmem_limit_bytes=V7X_VMEM_LIMIT_BYTES)


def _tile(n, pref):
    if n <= pref:
        return n
    t = pref
    while n % t:
        t //= 2
    return t


def _mx(a):
    return a.astype(MXU_DTYPE)


def _dot(a, b):
    return jnp.dot(_mx(a), _mx(b), preferred_element_type=F32)


def _dot_nt(a, b):
    return lax.dot_general(_mx(a), _mx(b), (((1,), (1,)), ((), ())), preferred_element_type=F32)


def _dot_tn(a, b):
    return lax.dot_general(_mx(a), _mx(b), (((0,), (0,)), ((), ())), preferred_element_type=F32)


def _fused_matmul_kernel(*refs, n_lhs, lhs_of, n_extra, epilogue, rhs_t):
    n_rhs = len(lhs_of)
    lhs = refs[:n_lhs]
    rhs = refs[n_lhs:n_lhs + n_rhs]
    extra = refs[n_lhs + n_rhs:n_lhs + n_rhs + n_extra]
    outs = refs[n_lhs + n_rhs + n_extra:]
    dot = _dot_nt if rhs_t else _dot
    accs = [dot(lhs[j][...], b[...]) for j, b in zip(lhs_of, rhs)]
    res = epilogue(accs, [e[...] for e in extra])
    if not isinstance(res, (tuple, list)):
        res = (res,)
    for o, r in zip(outs, res):
        o[...] = r.astype(o.dtype)


def _fused_matmul(lhs, rhs, extras, epilogue, out_dtypes, n_cols, *, tm=1024, tn=512, rhs_t=False, name):
    m = lhs[0].shape[0]
    tm = _tile(m, tm)
    tn = _tile(n_cols, tn)
    in_specs = []
    for a in lhs:
        in_specs.append(pl.BlockSpec((tm, a.shape[1]), lambda n, i: (i, 0)))
    for (b, off, _) in rhs:
        assert off % tn == 0
        if rhs_t:
            in_specs.append(pl.BlockSpec((tn, b.shape[1]), functools.partial(lambda n, i, o: (n + o, 0), o=off // tn)))
        else:
            in_specs.append(pl.BlockSpec((b.shape[0], tn), functools.partial(lambda n, i, o: (0, n + o), o=off // tn)))
    for (e, off, *row_off) in extras:
        if off is None:
            in_specs.append(pl.BlockSpec((tm, e.shape[1]), lambda n, i: (i, 0)))
        else:
            r0 = row_off[0] if row_off else 0
            assert off % tn == 0 and r0 % tm == 0
            in_specs.append(pl.BlockSpec(
                (tm, tn), functools.partial(lambda n, i, o, r: (i + r, n + o), o=off // tn, r=r0 // tm)))
    out_shape = [jax.ShapeDtypeStruct((m, n_cols), dt) for dt in out_dtypes]
    out_specs = [pl.BlockSpec((tm, tn), lambda n, i: (i, n)) for _ in out_dtypes]
    kern = functools.partial(_fused_matmul_kernel, n_lhs=len(lhs), lhs_of=tuple(j for _, _, j in rhs),
                             n_extra=len(extras), epilogue=epilogue, rhs_t=rhs_t)
    res = pl.pallas_call(
        kern, grid=(n_cols // tn, m // tm), in_specs=in_specs, out_specs=out_specs, out_shape=out_shape,
        compiler_params=_cparams(("parallel", "arbitrary")), name=name,
    )(*lhs, *[b for b, _, _ in rhs], *[e[0] for e in extras])
    return res[0] if len(res) == 1 else res


def _rotary(t, cos, sin):
    half = t.shape[-1] // 2
    t1, t2 = t[:, :half], t[:, half:]
    return jnp.concatenate([t1 * cos - t2 * sin, t1 * sin + t2 * cos], axis=-1)


def _head_norm(y, g):
    mu = jnp.mean(y, axis=-1, keepdims=True)
    d = y - mu
    var = jnp.mean(d * d, axis=-1, keepdims=True)
    return d * lax.rsqrt(var + LN_EPS) * g


def _causal_conv_silu(u, hist_ref, w_ref, b_ref):
    cl = u.shape[0]
    n_hist = hist_ref.shape[0]
    taps = w_ref.shape[0]
    ext = jnp.concatenate([hist_ref[...], u], axis=0)
    acc = b_ref[...] + w_ref[taps - 1:taps, :] * u
    for s in range(1, taps):
        acc = acc + w_ref[taps - 1 - s:taps - s, :] * pltpu.roll(ext, s, axis=0)[n_hist:n_hist + cl]
    hist_ref[...] = u[cl - n_hist:cl, :]
    return jax.nn.silu(acc)


def _recurrence_kernel(rq_ref, rk_ref, rv_ref, rg_ref, mq_ref, mk_ref, mv_ref, mo_ref, cos_ref, sin_ref,
                       gcol_ref, grow_ref, dec_ref, qd_ref, kd_ref, sl_ref, gr_ref, gm_ref,
                       cwq_ref, cwk_ref, cbq_ref, cbk_ref, hq_in, hk_in,
                       s_in, c_in, n_in, m_in,
                       yr_ref, ym_ref, s_ref, c_ref, n_ref, m_ref, hq_ref, hk_ref, *, qk_scale):
    head = pl.program_id(1)
    chunk = pl.program_id(2)
    cl = rq_ref.shape[0]

    @pl.when(chunk == 0)
    def _():
        s_ref[...] = s_in[...]
        c_ref[...] = c_in[...]
        n_ref[...] = n_in[...]
        m_ref[...] = m_in[...]
        n_hist = hq_in.shape[0]
        for h_ref, h_in in ((hq_ref, hq_in), (hk_ref, hk_in)):
            h_ref[...] = jnp.zeros_like(h_ref)
            h_ref[h_ref.shape[0] - n_hist:, :] = h_in[...]

    cos = cos_ref[...]
    sin = sin_ref[...]

    q = _rotary(rq_ref[...], cos, sin)
    k = _rotary(rk_ref[...], cos, sin) * qk_scale
    v = rv_ref[...]
    s_old = s_ref[...]
    scores = _dot_nt(q, k) * dec_ref[...]
    yr = _dot(scores, v) + _dot(q, s_old) * qd_ref[...]
    s_ref[...] = sl_ref[...] * s_old + _dot_tn(k * kd_ref[...], v)
    yr_ref[...] = (_head_norm(yr, gr_ref[...]) * jax.nn.silu(rg_ref[...])).astype(yr_ref.dtype)

    lane = lax.broadcasted_iota(jnp.int32, gcol_ref.shape, 1)
    gcol = gcol_ref[...]
    b_col = jnp.sum(jnp.where(lane == head, gcol, 0.0), axis=-1, keepdims=True)
    i_col = jnp.sum(jnp.where(lane == head + N_HEADS, gcol, 0.0), axis=-1, keepdims=True)
    b_row = grow_ref[0:1, :]
    i_row = grow_ref[1:2, :]
    m_prev = m_ref[0:1, 0:1]
    t_idx = lax.broadcasted_iota(jnp.int32, (cl, cl), 0)
    s_idx = lax.broadcasted_iota(jnp.int32, (cl, cl), 1)
    dmat = jnp.where(t_idx >= s_idx, b_col - b_row + i_row, -jnp.inf)
    a_col = b_col + m_prev
    m_t = jnp.maximum(a_col, jnp.max(dmat, axis=-1, keepdims=True))
    w_inter = jnp.exp(a_col - m_t)
    w_intra = jnp.exp(dmat - m_t)
    mq = _causal_conv_silu(mq_ref[...], hq_ref, cwq_ref, cbq_ref)
    mk = _causal_conv_silu(mk_ref[...], hk_ref, cwk_ref, cbk_ref) * qk_scale
    mv = mv_ref[...]
    c_old = c_ref[...]
    n_old = n_ref[...]
    qk = _dot_nt(mq, mk) * w_intra
    num = _dot(qk, mv) + w_inter * _dot(mq, c_old)
    den = jnp.sum(qk, axis=-1, keepdims=True) + w_inter * jnp.sum(mq * n_old, axis=-1, keepdims=True)
    ym = num / jnp.maximum(jnp.abs(den), jnp.exp(-m_t))
    m_last = m_t[cl - 1:cl, :]
    wl_inter = w_inter[cl - 1:cl, :]
    wl_col = jnp.exp(b_col[cl - 1:cl, :] - b_col + i_col - m_last)
    c_ref[...] = wl_inter * c_old + _dot_tn(mk * wl_col, mv)
    n_ref[...] = wl_inter * n_old + jnp.sum(mk * wl_col, axis=0, keepdims=True)
    m_ref[...] = jnp.broadcast_to(m_last, m_ref.shape)
    ym_ref[...] = (_head_norm(ym, gm_ref[...]) * jax.nn.sigmoid(mo_ref[...])).astype(ym_ref.dtype)


def _recurrences(z_main, cos, sin, gcol, conv_w, conv_b, grow, conv0, s0, c0, n0, m0, *, g_ret, g_mlstm, row0, n_seq,
                 seq_len, cl, offs, name):
    hd = s0.shape[-1]
    w = N_HEADS * hd
    taps = conv_w.shape[0]
    n_chunks = seq_len // cl
    rb0 = row0 // cl
    cb = {k: v // hd for k, v in offs.items()}

    def rows(b, c):
        return rb0 + b * n_chunks + c

    def zspec(col_key):
        return pl.BlockSpec((cl, hd), functools.partial(lambda b, h, c, o: (rows(b, c), h + o), o=cb[col_key]))

    log_g = jnp.log1p(-jnp.exp2(-5.0 - jnp.arange(N_HEADS, dtype=F32)))
    t = jnp.arange(cl, dtype=F32)
    diff = t[:, None] - t[None, :]
    dec = jnp.exp(jnp.where((diff >= 0)[None], diff[None] * log_g[:, None, None], -jnp.inf))
    qd = jnp.broadcast_to(jnp.exp((t[None, :] + 1.0) * log_g[:, None])[:, :, None], (N_HEADS, cl, hd))
    kd = jnp.broadcast_to(jnp.exp((cl - 1.0 - t)[None, :] * log_g[:, None])[:, :, None], (N_HEADS, cl, hd))
    sl = jnp.broadcast_to(jnp.exp(cl * log_g)[:, None, None], (N_HEADS, 1, hd))

    head_const = lambda shape: pl.BlockSpec((None,) + shape, lambda b, h, c: (h, 0, 0))
    state = lambda shape: pl.BlockSpec((None, None) + shape, lambda b, h, c: (b, h, 0, 0))
    in_specs = [
        zspec("rq"), zspec("rk"), zspec("rv"), zspec("rg"), zspec("mq"), zspec("mk"), zspec("mv"), zspec("mo"),
        pl.BlockSpec((cl, hd // 2), lambda b, h, c: (rows(b, c), 0)),
        pl.BlockSpec((cl, hd // 2), lambda b, h, c: (rows(b, c), 0)),
        pl.BlockSpec((cl, 128), lambda b, h, c: (rows(b, c), 0)),
        pl.BlockSpec((None, None, 8, cl), lambda b, h, c: (b * n_chunks + c, h, 0, 0)),
        head_const((cl, cl)), head_const((cl, hd)), head_const((cl, hd)), head_const((1, hd)),
        pl.BlockSpec((1, hd), lambda b, h, c: (0, h)), pl.BlockSpec((1, hd), lambda b, h, c: (0, h)),
        pl.BlockSpec((taps, hd), lambda b, h, c: (0, h)), pl.BlockSpec((taps, hd), lambda b, h, c: (0, h + N_HEADS)),
        pl.BlockSpec((1, hd), lambda b, h, c: (0, h)), pl.BlockSpec((1, hd), lambda b, h, c: (0, h + N_HEADS)),
        pl.BlockSpec((None, taps - 1, hd), lambda b, h, c: (b, 0, h)),
        pl.BlockSpec((None, taps - 1, hd), lambda b, h, c: (b, 0, h + N_HEADS)),
        state((hd, hd)), state((hd, hd)), state((1, hd)), state((1, 128)),
    ]
    n_rows = n_seq * seq_len
    out_shape = [
        jax.ShapeDtypeStruct((n_rows, w), MXU_DTYPE), jax.ShapeDtypeStruct((n_rows, w), MXU_DTYPE),
        jax.ShapeDtypeStruct((n_seq, N_HEADS, hd, hd), F32), jax.ShapeDtypeStruct((n_seq, N_HEADS, hd, hd), F32),
        jax.ShapeDtypeStruct((n_seq, N_HEADS, 1, hd), F32), jax.ShapeDtypeStruct((n_seq, N_HEADS, 1, 128), F32),
    ]
    y_spec = pl.BlockSpec((cl, hd), lambda b, h, c: (b * n_chunks + c, h))
    out_specs = [y_spec, y_spec, state((hd, hd)), state((hd, hd)), state((1, hd)), state((1, 128))]
    kern = functools.partial(_recurrence_kernel, qk_scale=float(hd) ** -0.5)
    hist_rows = 8
    assert taps - 1 <= hist_rows <= cl
    return pl.pallas_call(
        kern, grid=(n_seq, N_HEADS, n_chunks), in_specs=in_specs, out_specs=out_specs, out_shape=out_shape,
        scratch_shapes=[pltpu.VMEM((hist_rows, hd), F32), pltpu.VMEM((hist_rows, hd), F32)],
        compiler_params=_cparams(("parallel", "parallel", "arbitrary")), name=name,
    )(z_main, z_main, z_main, z_main, z_main, z_main, z_main, z_main, cos, sin, gcol, grow, dec, qd, kd, sl,
      g_ret, g_mlstm, conv_w, conv_w, conv_b, conv_b, conv0, conv0, s0, c0, n0, m0)


def _layer_norm_rows(u, g, b):
    mu = jnp.mean(u, axis=-1, keepdims=True)
    d = u - mu
    var = jnp.mean(d * d, axis=-1, keepdims=True)
    return d * lax.rsqrt(var + LN_EPS) * g + b


def _first_argmax(vals, lane_f, n):
    mx = jnp.max(vals, axis=-1, keepdims=True)
    return mx, jnp.min(jnp.where(vals == mx, lane_f, float(n)), axis=-1, keepdims=True)


def _route_rows(logits, bias):
    rows, e = logits.shape
    gsz = e // N_GROUPS
    lane = lax.broadcasted_iota(jnp.int32, (rows, e), 1)
    lane_f = lane.astype(F32)
    scores = jax.nn.sigmoid(logits)
    sel = scores + bias
    in_grp = [(lane >= g * gsz) & (lane < (g + 1) * gsz) for g in range(N_GROUPS)]
    grp_score = []
    for g in range(N_GROUPS):
        m = jnp.where(in_grp[g], sel, -jnp.inf)
        m1, i1 = _first_argmax(m, lane_f, e)
        m2 = jnp.max(jnp.where(lane_f == i1, -jnp.inf, m), axis=-1, keepdims=True)
        grp_score.append(m1 + m2)
    keep = jnp.zeros((rows, e), F32)
    for g in range(N_GROUPS):
        rank = jnp.zeros_like(grp_score[g])
        for o in range(N_GROUPS):
            if o != g:
                ahead = (grp_score[o] >= grp_score[g]) if o < g else (grp_score[o] > grp_score[g])
                rank = rank + ahead.astype(F32)
        keep = jnp.where(in_grp[g], (rank < TOPK_GROUPS).astype(F32), keep)
    cur = jnp.where(keep > 0.0, sel, -jnp.inf)
    ids, top_s = [], []
    onehot = jnp.zeros((rows, e), F32)
    for _ in range(TOP_K):
        _, ik = _first_argmax(cur, lane_f, e)
        hit = lane_f == ik
        top_s.append(jnp.sum(jnp.where(hit, scores, 0.0), axis=-1, keepdims=True))
        cur = jnp.where(hit, -jnp.inf, cur)
        onehot = jnp.where(hit, 1.0, onehot)
        ids.append(ik)
    den = top_s[0]
    for s in top_s[1:]:
        den = den + s
    gates = [s / den * ROUTED_SCALE for s in top_s]
    return ids, gates, onehot


def _pack_lanes(cols, width):
    rows = cols[0].shape[0]
    lane = lax.broadcasted_iota(jnp.int32, (rows, width), 1)
    out = jnp.zeros((rows, width), cols[0].dtype)
    for k, c in enumerate(cols):
        out = jnp.where(lane == k, c, out)
    return out


def _ln_router_kernel(u_ref, g_ref, b_ref, wr_ref, br_ref, h_ref, hb_ref, idx_ref, gate_ref, cnt_ref):
    h = _layer_norm_rows(u_ref[...], g_ref[...], b_ref[...])
    h_ref[...] = h
    hb = _mx(h)
    hb_ref[...] = hb
    logits = jnp.dot(hb, _mx(wr_ref[...]), preferred_element_type=F32)
    ids, gates, onehot = _route_rows(logits, br_ref[...])
    idx_ref[...] = _pack_lanes(ids, idx_ref.shape[1]).astype(jnp.int32)
    gate_ref[...] = _pack_lanes(gates, gate_ref.shape[1])
    cnt_ref[...] = jnp.broadcast_to(jnp.sum(onehot, axis=0, keepdims=True), cnt_ref.shape)


def _ln_router(u, g, b, w_router, b_router):
    t, d = u.shape
    e = w_router.shape[1]
    tr = _tile(t, 256)
    nt = t // tr
    row = pl.BlockSpec((tr, d), lambda i: (i, 0))
    vec = pl.BlockSpec((1, d), lambda i: (0, 0))
    lanes = pl.BlockSpec((tr, 128), lambda i: (i, 0))
    return pl.pallas_call(
        _ln_router_kernel, grid=(nt,),
        in_specs=[row, vec, vec, pl.BlockSpec((d, e), lambda i: (0, 0)), pl.BlockSpec((1, e), lambda i: (0, 0))],
        out_specs=[row, row, lanes, lanes, pl.BlockSpec((None, 8, e), lambda i: (i, 0, 0))],
        out_shape=[jax.ShapeDtypeStruct((t, d), F32), jax.ShapeDtypeStruct((t, d), MXU_DTYPE),
                   jax.ShapeDtypeStruct((t, 128), jnp.int32), jax.ShapeDtypeStruct((t, 128), F32),
                   jax.ShapeDtypeStruct((nt, 8, e), F32)],
        compiler_params=_cparams(("parallel",)), name="ln1_router",
    )(u, g, b, w_router, b_router)


def _dest_kernel(idx_ref, start_ref, *refs):
    dest_refs, carry_ref = refs[:-1], refs[-1]

    @pl.when(pl.program_id(0) == 0)
    def _():
        carry_ref[...] = jnp.zeros_like(carry_ref)

    rows = idx_ref.shape[0]
    e = start_ref.shape[1]
    idx = idx_ref[...].astype(F32)
    lane_f = lax.broadcasted_iota(jnp.int32, (rows, e), 1).astype(F32)
    hits = [lane_f == idx[:, k:k + 1] for k in range(TOP_K)]
    onehot = jnp.zeros((rows, e), F32)
    for hit in hits:
        onehot = jnp.where(hit, 1.0, onehot)
    r = lax.broadcasted_iota(jnp.int32, (rows, rows), 0)
    c = lax.broadcasted_iota(jnp.int32, (rows, rows), 1)
    earlier = jnp.where(c < r, 1.0, 0.0)
    rank = _dot(earlier, onehot) + carry_ref[0:1, :]
    for j, dest_ref in enumerate(dest_refs):
        base = rank + start_ref[j:j + 1, :]
        dest = [jnp.sum(jnp.where(hit, base, 0.0), axis=-1, keepdims=True) for hit in hits]
        dest_ref[...] = _pack_lanes(dest, dest_ref.shape[1]).astype(jnp.int32)
    carry_ref[...] = carry_ref[...] + jnp.sum(onehot, axis=0, keepdims=True)


def _dest_rows(idx, starts):
    t = idx.shape[0]
    n_layouts, e = starts.shape
    starts = jnp.pad(starts.astype(F32), ((0, 8 - n_layouts), (0, 0)))
    tr = _tile(t, 256)
    lanes = pl.BlockSpec((tr, 128), lambda i: (i, 0))
    return pl.pallas_call(
        _dest_kernel, grid=(t // tr,),
        in_specs=[lanes, pl.BlockSpec((8, e), lambda i: (0, 0))],
        out_specs=[lanes] * n_layouts,
        out_shape=[jax.ShapeDtypeStruct((t, 128), jnp.int32)] * n_layouts,
        scratch_shapes=[pltpu.VMEM((8, e), F32)],
        compiler_params=_cparams(("arbitrary",)), name="expert_rows",
    )(idx, starts)


def _ln_kernel(u_ref, g_ref, b_ref, x_ref, xb_ref):
    x2 = _layer_norm_rows(u_ref[...], g_ref[...], b_ref[...])
    x_ref[...] = x2
    xb_ref[...] = _mx(x2)


def _ln(u, g, b):
    t, d = u.shape
    tr = _tile(t, 256)
    row = pl.BlockSpec((tr, d), lambda i: (i, 0))
    vec = pl.BlockSpec((1, d), lambda i: (0, 0))
    return pl.pallas_call(
        _ln_kernel, grid=(t // tr,), in_specs=[row, vec, vec], out_specs=[row, row],
        out_shape=[jax.ShapeDtypeStruct((t, d), F32), jax.ShapeDtypeStruct((t, d), MXU_DTYPE)],
        compiler_params=_cparams(("parallel",)), name="ln2",
    )(u, g, b)


def _expert_kernel(be_ref, bx_ref, x_ref, wg_ref, wu_ref, wd_ref, y_ref, acc_ref):
    f = pl.program_id(1)
    x = x_ref[...]
    g = jnp.dot(x, _mx(wg_ref[...]), preferred_element_type=F32)
    u = jnp.dot(x, _mx(wu_ref[...]), preferred_element_type=F32)
    part = _dot(jax.nn.silu(g) * u, wd_ref[...])

    @pl.when(f == 0)
    def _():
        acc_ref[...] = part

    @pl.when(f > 0)
    def _():
        acc_ref[...] += part

    @pl.when(f == pl.num_programs(1) - 1)
    def _():
        y_ref[...] = acc_ref[...].astype(y_ref.dtype)


def _grouped_experts(x_sorted, block_expert, block_xrow, n_used, n_blocks, w_gate, w_up, w_down):
    d = x_sorted.shape[1]
    dff = w_gate.shape[-1]
    tf = _tile(dff, EXPERT_FF_TILE)
    nf = dff // tf
    grid_spec = pltpu.PrefetchScalarGridSpec(
        num_scalar_prefetch=2, grid=(n_used, nf),
        in_specs=[
            pl.BlockSpec((pl.Element(EXPERT_ROWS), pl.Element(d)),
                         lambda b, f, be, bx: (pl.multiple_of(bx[b], X_ROW_ALIGN), 0)),
            pl.BlockSpec((None, d, tf), lambda b, f, be, bx: (be[b], 0, f)),
            pl.BlockSpec((None, d, tf), lambda b, f, be, bx: (be[b], 0, f)),
            pl.BlockSpec((None, tf, d), lambda b, f, be, bx: (be[b], f, 0)),
        ],
        out_specs=pl.BlockSpec((EXPERT_ROWS, d), lambda b, f, be, bx: (b, 0)),
        scratch_shapes=[pltpu.VMEM((EXPERT_ROWS, d), F32)],
    )
    return pl.pallas_call(
        _expert_kernel, grid_spec=grid_spec,
        out_shape=jax.ShapeDtypeStruct((n_blocks * EXPERT_ROWS, d), MXU_DTYPE),
        compiler_params=_cparams(("arbitrary", "arbitrary")), name="grouped_experts",
    )(block_expert, block_xrow, x_sorted, w_gate, w_up, w_down)


def _routed_experts(hb, idx, counts, w_gate, w_up, w_down):
    t, d = hb.shape
    e = w_gate.shape[0]
    a = t * TOP_K
    counts = counts.astype(jnp.int32)
    blocks_per = (counts + EXPERT_ROWS - 1) // EXPERT_ROWS
    blk_end = jnp.cumsum(blocks_per)
    blk_first = blk_end - blocks_per
    y_start = blk_first * EXPERT_ROWS
    x_rows = (counts + X_ROW_ALIGN - 1) // X_ROW_ALIGN * X_ROW_ALIGN
    x_start = jnp.cumsum(x_rows) - x_rows
    n_blocks = -(-a // EXPERT_ROWS) + e
    blocks = jnp.arange(n_blocks, dtype=jnp.int32)
    block_expert = jnp.minimum(jnp.searchsorted(blk_end, blocks, side="right"), e - 1).astype(jnp.int32)
    block_xrow = (x_start[block_expert] + (blocks - blk_first[block_expert]) * EXPERT_ROWS).astype(jnp.int32)
    n_used = blk_end[-1].astype(jnp.int32)
    dest_x, dest_y = _dest_rows(idx, jnp.stack([x_start, y_start]))
    px = a + X_ROW_ALIGN * e + EXPERT_ROWS
    slot_tok = (jnp.arange(px, dtype=jnp.int32) % t).at[dest_x[:, :TOP_K].reshape(a)].set(
        jnp.arange(a, dtype=jnp.int32) // TOP_K)
    x_sorted = hb[slot_tok]
    y = _grouped_experts(x_sorted, block_expert, block_xrow, n_used, n_blocks, w_gate, w_up, w_down)
    return y[dest_y[:, :TOP_K].T.reshape(a)]


def kernel(x_prompt, x_sample, state_ret, state_mlstm_c, state_mlstm_n, state_mlstm_m, state_conv, p_prompt, p_sample, w_in, b_gate_i, b_gate_f, conv_w, conv_b, g_ret, g_mlstm, w_br_r, w_br_m, w_out, ln1_g, ln1_b, w_router, b_router, w_e_gate, w_e_up, w_e_down, w_s_gate, w_s_up, w_s_down, ln2_g, ln2_b, w_pe, w_pg):
    depth = w_in.shape[0]
    assert depth == 1, "single-layer step"
    alpha = (2 * depth) ** 0.25
    bp, lp, d = x_prompt.shape
    bs, ls, _ = x_sample.shape
    assert bp == 1
    hd = state_ret.shape[-1]
    w = N_HEADS * hd
    conv_hist = state_conv.shape[2]
    off = {"rq": 0, "rk": w, "rv": 2 * w, "rg": 3 * w, "mq": 4 * w, "mk": 5 * w, "mv": 6 * w, "mo": 7 * w}
    off_mi = 8 * w
    n_main = 8 * w
    tp, ts = bp * lp, bs * ls
    t = tp + ts

    x = jnp.concatenate([x_prompt.reshape(tp, d), x_sample.reshape(ts, d)], axis=0)
    xb = _mx(x)
    w_in_t = jnp.swapaxes(w_in[0], 0, 1)
    first = lambda accs, extras: accs[0]

    z_main = _fused_matmul([xb], [(w_in_t, 0, 0)], [], first, [F32], n_main, rhs_t=True, name="proj_main")
    tn_tail = _tile(d, 512)
    n_tail = -(-(2 * d + 2 * N_HEADS) // tn_tail) * tn_tail
    z_tail = _fused_matmul([xb], [(w_in_t, off_mi, 0)], [], first, [F32], n_tail, tn=tn_tail, rhs_t=True,
                           name="proj_tail")
    z_if = z_tail[:, :2 * N_HEADS]

    def conv_tail(hist, rows, n_seq, length):
        tail = z_main[rows].reshape(n_seq, length, -1)[:, -conv_hist:, off["mq"]:off["mv"]]
        return jnp.concatenate([hist, tail], axis=1)[:, -conv_hist:][None]

    conv0_p = jnp.zeros((bp, conv_hist, 2 * w), F32)
    conv_p = conv_tail(conv0_p, slice(0, tp), bp, lp)
    conv_s = conv_tail(state_conv[0], slice(tp, t), bs, ls)

    pos = jnp.concatenate([jnp.tile(jnp.arange(lp, dtype=jnp.int32), bp),
                           jnp.tile(PAST_LEN + jnp.arange(ls, dtype=jnp.int32), bs)])
    half = hd // 2
    inv_freq = ROPE_BASE ** (-jnp.arange(half, dtype=F32) / half)
    ang = pos.astype(F32)[:, None] * inv_freq[None, :]
    cos, sin = jnp.cos(ang), jnp.sin(ang)

    mi = z_if[:, :N_HEADS] + b_gate_i[0]
    lf = jax.nn.log_sigmoid(z_if[:, N_HEADS:2 * N_HEADS] + b_gate_f[0])
    cl_p = _tile(lp, PROMPT_CHUNK)
    cl_s = ls

    def gate_layouts(rows, cl):
        b = jnp.cumsum(lf[rows].reshape(-1, cl, N_HEADS), axis=1)
        i = mi[rows].reshape(-1, cl, N_HEADS)
        col = jnp.concatenate([b, i], axis=-1).reshape(-1, 2 * N_HEADS)
        row = jnp.stack([jnp.swapaxes(b, 1, 2), jnp.swapaxes(i, 1, 2)], axis=2)
        return jnp.pad(col, ((0, 0), (0, 128 - 2 * N_HEADS))), jnp.pad(row, ((0, 0), (0, 0), (0, 6), (0, 0)))

    gcol_p, grow_p = gate_layouts(slice(0, tp), cl_p)
    gcol_s, grow_s = gate_layouts(slice(tp, t), cl_s)
    gcol = jnp.concatenate([gcol_p, gcol_s], axis=0)

    zeros = lambda *shape: jnp.zeros(shape, F32)
    rec = functools.partial(_recurrences, z_main, cos, sin, gcol, conv_w[0], conv_b, g_ret=g_ret, g_mlstm=g_mlstm,
                            offs=off)
    yr_p, ym_p, s_p, c_p, n_p, m_p = rec(
        grow_p, conv0_p, s0=zeros(bp, N_HEADS, hd, hd), c0=zeros(bp, N_HEADS, hd, hd),
        n0=zeros(bp, N_HEADS, 1, hd), m0=zeros(bp, N_HEADS, 1, 128), row0=0, n_seq=bp, seq_len=lp, cl=cl_p,
        name="recurrence_prompt")
    m0_s = jnp.broadcast_to(state_mlstm_m[0][:, :, None, None], (bs, N_HEADS, 1, 128))
    yr_s, ym_s, s_s, c_s, n_s, m_s = rec(
        grow_s, state_conv[0], s0=state_ret[0], c0=state_mlstm_c[0], n0=state_mlstm_n[0][:, :, None, :], m0=m0_s,
        row0=tp, n_seq=bs, seq_len=ls, cl=cl_s, name="recurrence_sample")
    ret_out = jnp.concatenate([yr_p, yr_s], axis=0)
    mls_out = jnp.concatenate([ym_p, ym_s], axis=0)

    def merge(accs, extras):
        sh = 2 * N_HEADS
        ga = jnp.concatenate([extras[0][:, sh:], extras[1][:, :sh]], axis=1)
        gb = jnp.concatenate([extras[2][:, sh:], extras[3][:, :sh]], axis=1)
        return jax.nn.sigmoid(ga) * accs[0] + jax.nn.sigmoid(gb) * accs[1]

    merged = _fused_matmul([ret_out, mls_out], [(w_br_r[0], 0, 0), (w_br_m[0], 0, 1)],
                           [(z_tail, 0), (z_tail, tn_tail), (z_tail, d), (z_tail, d + tn_tail)],
                           merge, [MXU_DTYPE], d, tm=512, tn=tn_tail, name="merge")

    u1 = _fused_matmul([merged], [(w_out[0], 0, 0)], [(x, 0)], lambda accs, extras: alpha * extras[0] + accs[0],
                       [F32], d, name="out_proj")
    h, hb, idx, gates, cnt = _ln_router(u1, ln1_g, ln1_b, w_router[0], b_router)

    yk = _routed_experts(hb, idx, cnt[:, 0, :].sum(0), w_e_gate[0], w_e_up[0], w_e_down[0])
    act = _fused_matmul([hb], [(w_s_gate[0], 0, 0), (w_s_up[0], 0, 0)], [],
                        lambda accs, extras: jax.nn.silu(accs[0]) * accs[1], [MXU_DTYPE], w_s_gate.shape[-1],
                        tm=512, tn=256, name="shared_up")

    def ffn_residual(accs, extras):
        routed = extras[2].astype(F32) * extras[1][:, 0:1]
        for k in range(1, TOP_K):
            routed = routed + extras[2 + k].astype(F32) * extras[1][:, k:k + 1]
        return alpha * extras[0] + (routed + accs[0])

    u2 = _fused_matmul([act], [(w_s_down[0], 0, 0)], [(h, 0), (gates, None)] + [(yk, 0, k * t) for k in range(TOP_K)],
                       ffn_residual, [F32], d, tm=512, name="shared_down")

    x2, x2b = _ln(u2, ln2_g, ln2_b)
    p = _mx(jnp.concatenate([p_prompt[0].reshape(tp, -1), p_sample[0].reshape(ts, -1)], axis=0))
    out = _fused_matmul([x2b, p], [(w_pg[0], 0, 0), (w_pe[0], 0, 1)], [(x2, 0)],
                        lambda accs, extras: extras[0] + jax.nn.sigmoid(accs[0]) * accs[1], [F32], d, tm=512,
                        name="embed_gate")

    dt = x_prompt.dtype
    y_p = out[:tp].reshape(bp, lp, d).astype(dt)
    y_s = out[tp:].reshape(bs, ls, d).astype(dt)
    return (y_p, y_s,
            s_p[None].astype(dt), c_p[None].astype(dt), n_p[:, :, 0][None].astype(dt), m_p[:, :, 0, 0][None].astype(dt),
            conv_p.astype(dt),
            s_s[None].astype(dt), c_s[None].astype(dt), n_s[:, :, 0][None].astype(dt), m_s[:, :, 0, 0][None].astype(dt),
            conv_s.astype(dt))
```

```python
import functools

import jax
import jax.numpy as jnp
from jax import lax
from jax.experimental import pallas as pl
from jax.experimental.pallas import tpu as pltpu

F32 = jnp.float32
MXU_DTYPE = jnp.bfloat16

N_HEADS = 8
PAST_LEN = 1024
TOP_K = 8
N_GROUPS = 8
TOPK_GROUPS = 4
ROUTED_SCALE = 2.5
ROPE_BASE = 10000.0
LN_EPS = 1e-5

V7X_VMEM_LIMIT_BYTES = 56 * 1024 * 1024
PROMPT_CHUNK = 256
EXPERT_ROWS = 384
EXPERT_FF_TILE = 256
X_ROW_ALIGN = 16


def _cparams(sem):
    return pltpu.CompilerParams(dimension_semantics=sem, vmem_limit_bytes=V7X_VMEM_LIMIT_BYTES)


def _tile(n, pref):
    if n <= pref:
        return n
    t = pref
    while n % t:
        t //= 2
    return t


def _mx(a):
    return a.astype(MXU_DTYPE)


def _dot(a, b):
    return jnp.dot(_mx(a), _mx(b), preferred_element_type=F32)


def _dot_nt(a, b):
    return lax.dot_general(_mx(a), _mx(b), (((1,), (1,)), ((), ())), preferred_element_type=F32)


def _dot_tn(a, b):
    return lax.dot_general(_mx(a), _mx(b), (((0,), (0,)), ((), ())), preferred_element_type=F32)


def _fused_matmul_kernel(*refs, n_lhs, lhs_of, n_extra, epilogue, rhs_t):
    n_rhs = len(lhs_of)
    lhs = refs[:n_lhs]
    rhs = refs[n_lhs:n_lhs + n_rhs]
    extra = refs[n_lhs + n_rhs:n_lhs + n_rhs + n_extra]
    outs = refs[n_lhs + n_rhs + n_extra:]
    dot = _dot_nt if rhs_t else _dot
    accs = [dot(lhs[j][...], b[...]) for j, b in zip(lhs_of, rhs)]
    res = epilogue(accs, [e[...] for e in extra])
    if not isinstance(res, (tuple, list)):
        res = (res,)
    for o, r in zip(outs, res):
        o[...] = r.astype(o.dtype)


def _fused_matmul(lhs, rhs, extras, epilogue, out_dtypes, n_cols, *, tm=1024, tn=512, rhs_t=False, name):
    m = lhs[0].shape[0]
    tm = _tile(m, tm)
    tn = _tile(n_cols, tn)
    in_specs = []
    for a in lhs:
        in_specs.append(pl.BlockSpec((tm, a.shape[1]), lambda n, i: (i, 0)))
    for (b, off, _) in rhs:
        assert off % tn == 0
        if rhs_t:
            in_specs.append(pl.BlockSpec((tn, b.shape[1]), functools.partial(lambda n, i, o: (n + o, 0), o=off // tn)))
        else:
            in_specs.append(pl.BlockSpec((b.shape[0], tn), functools.partial(lambda n, i, o: (0, n + o), o=off // tn)))
    for (e, off, *row_off) in extras:
        if off is None:
            in_specs.append(pl.BlockSpec((tm, e.shape[1]), lambda n, i: (i, 0)))
        else:
            r0 = row_off[0] if row_off else 0
            assert off % tn == 0 and r0 % tm == 0
            in_specs.append(pl.BlockSpec(
                (tm, tn), functools.partial(lambda n, i, o, r: (i + r, n + o), o=off // tn, r=r0 // tm)))
    out_shape = [jax.ShapeDtypeStruct((m, n_cols), dt) for dt in out_dtypes]
    out_specs = [pl.BlockSpec((tm, tn), lambda n, i: (i, n)) for _ in out_dtypes]
    kern = functools.partial(_fused_matmul_kernel, n_lhs=len(lhs), lhs_of=tuple(j for _, _, j in rhs),
                             n_extra=len(extras), epilogue=epilogue, rhs_t=rhs_t)
    res = pl.pallas_call(
        kern, grid=(n_cols // tn, m // tm), in_specs=in_specs, out_specs=out_specs, out_shape=out_shape,
        compiler_params=_cparams(("parallel", "arbitrary")), name=name,
    )(*lhs, *[b for b, _, _ in rhs], *[e[0] for e in extras])
    return res[0] if len(res) == 1 else res


def _rotary(t, cos, sin):
    half = t.shape[-1] // 2
    t1, t2 = t[:, :half], t[:, half:]
    return jnp.concatenate([t1 * cos - t2 * sin, t1 * sin + t2 * cos], axis=-1)


def _head_norm(y, g):
    mu = jnp.mean(y, axis=-1, keepdims=True)
    d = y - mu
    var = jnp.mean(d * d, axis=-1, keepdims=True)
    return d * lax.rsqrt(var + LN_EPS) * g


def _causal_conv_silu(u, hist_ref, w_ref, b_ref):
    cl = u.shape[0]
    n_hist = hist_ref.shape[0]
    taps = w_ref.shape[0]
    ext = jnp.concatenate([hist_ref[...], u], axis=0)
    acc = b_ref[...] + w_ref[taps - 1:taps, :] * u
    for s in range(1, taps):
        acc = acc + w_ref[taps - 1 - s:taps - s, :] * pltpu.roll(ext, s, axis=0)[n_hist:n_hist + cl]
    hist_ref[...] = u[cl - n_hist:cl, :]
    return jax.nn.silu(acc)


def _recurrence_kernel(rq_ref, rk_ref, rv_ref, rg_ref, mq_ref, mk_ref, mv_ref, mo_ref, cos_ref, sin_ref,
                       gcol_ref, grow_ref, dec_ref, qd_ref, kd_ref, sl_ref, gr_ref, gm_ref,
                       cwq_ref, cwk_ref, cbq_ref, cbk_ref, hq_in, hk_in,
                       s_in, c_in, n_in, m_in,
                       yr_ref, ym_ref, s_ref, c_ref, n_ref, m_ref, hq_ref, hk_ref, *, qk_scale):
    head = pl.program_id(1)
    chunk = pl.program_id(2)
    cl = rq_ref.shape[0]

    @pl.when(chunk == 0)
    def _():
        s_ref[...] = s_in[...]
        c_ref[...] = c_in[...]
        n_ref[...] = n_in[...]
        m_ref[...] = m_in[...]
        n_hist = hq_in.shape[0]
        for h_ref, h_in in ((hq_ref, hq_in), (hk_ref, hk_in)):
            h_ref[...] = jnp.zeros_like(h_ref)
            h_ref[h_ref.shape[0] - n_hist:, :] = h_in[...]

    cos = cos_ref[...]
    sin = sin_ref[...]

    q = _rotary(rq_ref[...], cos, sin)
    k = _rotary(rk_ref[...], cos, sin) * qk_scale
    v = rv_ref[...]
    s_old = s_ref[...]
    scores = _dot_nt(q, k) * dec_ref[...]
    yr = _dot(scores, v) + _dot(q, s_old) * qd_ref[...]
    s_ref[...] = sl_ref[...] * s_old + _dot_tn(k * kd_ref[...], v)
    yr_ref[...] = (_head_norm(yr, gr_ref[...]) * jax.nn.silu(rg_ref[...])).astype(yr_ref.dtype)

    lane = lax.broadcasted_iota(jnp.int32, gcol_ref.shape, 1)
    gcol = gcol_ref[...]
    b_col = jnp.sum(jnp.where(lane == head, gcol, 0.0), axis=-1, keepdims=True)
    i_col = jnp.sum(jnp.where(lane == head + N_HEADS, gcol, 0.0), axis=-1, keepdims=True)
    b_row = grow_ref[0:1, :]
    i_row = grow_ref[1:2, :]
    m_prev = m_ref[0:1, 0:1]
    t_idx = lax.broadcasted_iota(jnp.int32, (cl, cl), 0)
    s_idx = lax.broadcasted_iota(jnp.int32, (cl, cl), 1)
    dmat = jnp.where(t_idx >= s_idx, b_col - b_row + i_row, -jnp.inf)
    a_col = b_col + m_prev
    m_t = jnp.maximum(a_col, jnp.max(dmat, axis=-1, keepdims=True))
    w_inter = jnp.exp(a_col - m_t)
    w_intra = jnp.exp(dmat - m_t)
    mq = _causal_conv_silu(mq_ref[...], hq_ref, cwq_ref, cbq_ref)
    mk = _causal_conv_silu(mk_ref[...], hk_ref, cwk_ref, cbk_ref) * qk_scale
    mv = mv_ref[...]
    c_old = c_ref[...]
    n_old = n_ref[...]
    qk = _dot_nt(mq, mk) * w_intra
    num = _dot(qk, mv) + w_inter * _dot(mq, c_old)
    den = jnp.sum(qk, axis=-1, keepdims=True) + w_inter * jnp.sum(mq * n_old, axis=-1, keepdims=True)
    ym = num / jnp.maximum(jnp.abs(den), jnp.exp(-m_t))
    m_last = m_t[cl - 1:cl, :]
    wl_inter = w_inter[cl - 1:cl, :]
    wl_col = jnp.exp(b_col[cl - 1:cl, :] - b_col + i_col - m_last)
    c_ref[...] = wl_inter * c_old + _dot_tn(mk * wl_col, mv)
    n_ref[...] = wl_inter * n_old + jnp.sum(mk * wl_col, axis=0, keepdims=True)
    m_ref[...] = jnp.broadcast_to(m_last, m_ref.shape)
    ym_ref[...] = (_head_norm(ym, gm_ref[...]) * jax.nn.sigmoid(mo_ref[...])).astype(ym_ref.dtype)


def _recurrences(z_main, cos, sin, gcol, conv_w, conv_b, grow, conv0, s0, c0, n0, m0, *, g_ret, g_mlstm, row0, n_seq,
                 seq_len, cl, offs, name):
    hd = s0.shape[-1]
    w = N_HEADS * hd
    taps = conv_w.shape[0]
    n_chunks = seq_len // cl
    rb0 = row0 // cl
    cb = {k: v // hd for k, v in offs.items()}

    def rows(b, c):
        return rb0 + b * n_chunks + c

    def zspec(col_key):
        return pl.BlockSpec((cl, hd), functools.partial(lambda b, h, c, o: (rows(b, c), h + o), o=cb[col_key]))

    log_g = jnp.log1p(-jnp.exp2(-5.0 - jnp.arange(N_HEADS, dtype=F32)))
    t = jnp.arange(cl, dtype=F32)
    diff = t[:, None] - t[None, :]
    dec = jnp.exp(jnp.where((diff >= 0)[None], diff[None] * log_g[:, None, None], -jnp.inf))
    qd = jnp.broadcast_to(jnp.exp((t[None, :] + 1.0) * log_g[:, None])[:, :, None], (N_HEADS, cl, hd))
    kd = jnp.broadcast_to(jnp.exp((cl - 1.0 - t)[None, :] * log_g[:, None])[:, :, None], (N_HEADS, cl, hd))
    sl = jnp.broadcast_to(jnp.exp(cl * log_g)[:, None, None], (N_HEADS, 1, hd))

    head_const = lambda shape: pl.BlockSpec((None,) + shape, lambda b, h, c: (h, 0, 0))
    state = lambda shape: pl.BlockSpec((None, None) + shape, lambda b, h, c: (b, h, 0, 0))
    in_specs = [
        zspec("rq"), zspec("rk"), zspec("rv"), zspec("rg"), zspec("mq"), zspec("mk"), zspec("mv"), zspec("mo"),
        pl.BlockSpec((cl, hd // 2), lambda b, h, c: (rows(b, c), 0)),
        pl.BlockSpec((cl, hd // 2), lambda b, h, c: (rows(b, c), 0)),
        pl.BlockSpec((cl, 128), lambda b, h, c: (rows(b, c), 0)),
        pl.BlockSpec((None, None, 8, cl), lambda b, h, c: (b * n_chunks + c, h, 0, 0)),
        head_const((cl, cl)), head_const((cl, hd)), head_const((cl, hd)), head_const((1, hd)),
        pl.BlockSpec((1, hd), lambda b, h, c: (0, h)), pl.BlockSpec((1, hd), lambda b, h, c: (0, h)),
        pl.BlockSpec((taps, hd), lambda b, h, c: (0, h)), pl.BlockSpec((taps, hd), lambda b, h, c: (0, h + N_HEADS)),
        pl.BlockSpec((1, hd), lambda b, h, c: (0, h)), pl.BlockSpec((1, hd), lambda b, h, c: (0, h + N_HEADS)),
        pl.BlockSpec((None, taps - 1, hd), lambda b, h, c: (b, 0, h)),
        pl.BlockSpec((None, taps - 1, hd), lambda b, h, c: (b, 0, h + N_HEADS)),
        state((hd, hd)), state((hd, hd)), state((1, hd)), state((1, 128)),
    ]
    n_rows = n_seq * seq_len
    out_shape = [
        jax.ShapeDtypeStruct((n_rows, w), MXU_DTYPE), jax.ShapeDtypeStruct((n_rows, w), MXU_DTYPE),
        jax.ShapeDtypeStruct((n_seq, N_HEADS, hd, hd), F32), jax.ShapeDtypeStruct((n_seq, N_HEADS, hd, hd), F32),
        jax.ShapeDtypeStruct((n_seq, N_HEADS, 1, hd), F32), jax.ShapeDtypeStruct((n_seq, N_HEADS, 1, 128), F32),
    ]
    y_spec = pl.BlockSpec((cl, hd), lambda b, h, c: (b * n_chunks + c, h))
    out_specs = [y_spec, y_spec, state((hd, hd)), state((hd, hd)), state((1, hd)), state((1, 128))]
    kern = functools.partial(_recurrence_kernel, qk_scale=float(hd) ** -0.5)
    hist_rows = 8
    assert taps - 1 <= hist_rows <= cl
    return pl.pallas_call(
        kern, grid=(n_seq, N_HEADS, n_chunks), in_specs=in_specs, out_specs=out_specs, out_shape=out_shape,
        scratch_shapes=[pltpu.VMEM((hist_rows, hd), F32), pltpu.VMEM((hist_rows, hd), F32)],
        compiler_params=_cparams(("parallel", "parallel", "arbitrary")), name=name,
    )(z_main, z_main, z_main, z_main, z_main, z_main, z_main, z_main, cos, sin, gcol, grow, dec, qd, kd, sl,
      g_ret, g_mlstm, conv_w, conv_w, conv_b, conv_b, conv0, conv0, s0, c0, n0, m0)


def _layer_norm_rows(u, g, b):
    mu = jnp.mean(u, axis=-1, keepdims=True)
    d = u - mu
    var = jnp.mean(d * d, axis=-1, keepdims=True)
    return d * lax.rsqrt(var + LN_EPS) * g + b


def _first_argmax(vals, lane_f, n):
    mx = jnp.max(vals, axis=-1, keepdims=True)
    return mx, jnp.min(jnp.where(vals == mx, lane_f, float(n)), axis=-1, keepdims=True)


def _route_rows(logits, bias):
    rows, e = logits.shape
    gsz = e // N_GROUPS
    lane = lax.broadcasted_iota(jnp.int32, (rows, e), 1)
    lane_f = lane.astype(F32)
    scores = jax.nn.sigmoid(logits)
    sel = scores + bias
    in_grp = [(lane >= g * gsz) & (lane < (g + 1) * gsz) for g in range(N_GROUPS)]
    grp_score = []
    for g in range(N_GROUPS):
        m = jnp.where(in_grp[g], sel, -jnp.inf)
        m1, i1 = _first_argmax(m, lane_f, e)
        m2 = jnp.max(jnp.where(lane_f == i1, -jnp.inf, m), axis=-1, keepdims=True)
        grp_score.append(m1 + m2)
    keep = jnp.zeros((rows, e), F32)
    for g in range(N_GROUPS):
        rank = jnp.zeros_like(grp_score[g])
        for o in range(N_GROUPS):
            if o != g:
                ahead = (grp_score[o] >= grp_score[g]) if o < g else (grp_score[o] > grp_score[g])
                rank = rank + ahead.astype(F32)
        keep = jnp.where(in_grp[g], (rank < TOPK_GROUPS).astype(F32), keep)
    cur = jnp.where(keep > 0.0, sel, -jnp.inf)
    ids, top_s = [], []
    onehot = jnp.zeros((rows, e), F32)
    for _ in range(TOP_K):
        _, ik = _first_argmax(cur, lane_f, e)
        hit = lane_f == ik
        top_s.append(jnp.sum(jnp.where(hit, scores, 0.0), axis=-1, keepdims=True))
        cur = jnp.where(hit, -jnp.inf, cur)
        onehot = jnp.where(hit, 1.0, onehot)
        ids.append(ik)
    den = top_s[0]
    for s in top_s[1:]:
        den = den + s
    gates = [s / den * ROUTED_SCALE for s in top_s]
    return ids, gates, onehot


def _pack_lanes(cols, width):
    rows = cols[0].shape[0]
    lane = lax.broadcasted_iota(jnp.int32, (rows, width), 1)
    out = jnp.zeros((rows, width), cols[0].dtype)
    for k, c in enumerate(cols):
        out = jnp.where(lane == k, c, out)
    return out


def _ln_router_kernel(u_ref, g_ref, b_ref, wr_ref, br_ref, h_ref, hb_ref, idx_ref, gate_ref, cnt_ref):
    h = _layer_norm_rows(u_ref[...], g_ref[...], b_ref[...])
    h_ref[...] = h
    hb = _mx(h)
    hb_ref[...] = hb
    logits = jnp.dot(hb, _mx(wr_ref[...]), preferred_element_type=F32)
    ids, gates, onehot = _route_rows(logits, br_ref[...])
    idx_ref[...] = _pack_lanes(ids, idx_ref.shape[1]).astype(jnp.int32)
    gate_ref[...] = _pack_lanes(gates, gate_ref.shape[1])
    cnt_ref[...] = jnp.broadcast_to(jnp.sum(onehot, axis=0, keepdims=True), cnt_ref.shape)


def _ln_router(u, g, b, w_router, b_router):
    t, d = u.shape
    e = w_router.shape[1]
    tr = _tile(t, 256)
    nt = t // tr
    row = pl.BlockSpec((tr, d), lambda i: (i, 0))
    vec = pl.BlockSpec((1, d), lambda i: (0, 0))
    lanes = pl.BlockSpec((tr, 128), lambda i: (i, 0))
    return pl.pallas_call(
        _ln_router_kernel, grid=(nt,),
        in_specs=[row, vec, vec, pl.BlockSpec((d, e), lambda i: (0, 0)), pl.BlockSpec((1, e), lambda i: (0, 0))],
        out_specs=[row, row, lanes, lanes, pl.BlockSpec((None, 8, e), lambda i: (i, 0, 0))],
        out_shape=[jax.ShapeDtypeStruct((t, d), F32), jax.ShapeDtypeStruct((t, d), MXU_DTYPE),
                   jax.ShapeDtypeStruct((t, 128), jnp.int32), jax.ShapeDtypeStruct((t, 128), F32),
                   jax.ShapeDtypeStruct((nt, 8, e), F32)],
        compiler_params=_cparams(("parallel",)), name="ln1_router",
    )(u, g, b, w_router, b_router)


def _dest_kernel(idx_ref, start_ref, *refs):
    dest_refs, carry_ref = refs[:-1], refs[-1]

    @pl.when(pl.program_id(0) == 0)
    def _():
        carry_ref[...] = jnp.zeros_like(carry_ref)

    rows = idx_ref.shape[0]
    e = start_ref.shape[1]
    idx = idx_ref[...].astype(F32)
    lane_f = lax.broadcasted_iota(jnp.int32, (rows, e), 1).astype(F32)
    hits = [lane_f == idx[:, k:k + 1] for k in range(TOP_K)]
    onehot = jnp.zeros((rows, e), F32)
    for hit in hits:
        onehot = jnp.where(hit, 1.0, onehot)
    r = lax.broadcasted_iota(jnp.int32, (rows, rows), 0)
    c = lax.broadcasted_iota(jnp.int32, (rows, rows), 1)
    earlier = jnp.where(c < r, 1.0, 0.0)
    rank = _dot(earlier, onehot) + carry_ref[0:1, :]
    for j, dest_ref in enumerate(dest_refs):
        base = rank + start_ref[j:j + 1, :]
        dest = [jnp.sum(jnp.where(hit, base, 0.0), axis=-1, keepdims=True) for hit in hits]
        dest_ref[...] = _pack_lanes(dest, dest_ref.shape[1]).astype(jnp.int32)
    carry_ref[...] = carry_ref[...] + jnp.sum(onehot, axis=0, keepdims=True)


def _dest_rows(idx, starts):
    t = idx.shape[0]
    n_layouts, e = starts.shape
    starts = jnp.pad(starts.astype(F32), ((0, 8 - n_layouts), (0, 0)))
    tr = _tile(t, 256)
    lanes = pl.BlockSpec((tr, 128), lambda i: (i, 0))
    return pl.pallas_call(
        _dest_kernel, grid=(t // tr,),
        in_specs=[lanes, pl.BlockSpec((8, e), lambda i: (0, 0))],
        out_specs=[lanes] * n_layouts,
        out_shape=[jax.ShapeDtypeStruct((t, 128), jnp.int32)] * n_layouts,
        scratch_shapes=[pltpu.VMEM((8, e), F32)],
        compiler_params=_cparams(("arbitrary",)), name="expert_rows",
    )(idx, starts)


def _ln_kernel(u_ref, g_ref, b_ref, x_ref, xb_ref):
    x2 = _layer_norm_rows(u_ref[...], g_ref[...], b_ref[...])
    x_ref[...] = x2
    xb_ref[...] = _mx(x2)


def _ln(u, g, b):
    t, d = u.shape
    tr = _tile(t, 256)
    row = pl.BlockSpec((tr, d), lambda i: (i, 0))
    vec = pl.BlockSpec((1, d), lambda i: (0, 0))
    return pl.pallas_call(
        _ln_kernel, grid=(t // tr,), in_specs=[row, vec, vec], out_specs=[row, row],
        out_shape=[jax.ShapeDtypeStruct((t, d), F32), jax.ShapeDtypeStruct((t, d), MXU_DTYPE)],
        compiler_params=_cparams(("parallel",)), name="ln2",
    )(u, g, b)


def _expert_kernel(be_ref, bx_ref, x_ref, wg_ref, wu_ref, wd_ref, y_ref, acc_ref):
    f = pl.program_id(1)
    x = x_ref[...]
    g = jnp.dot(x, _mx(wg_ref[...]), preferred_element_type=F32)
    u = jnp.dot(x, _mx(wu_ref[...]), preferred_element_type=F32)
    part = _dot(jax.nn.silu(g) * u, wd_ref[...])

    @pl.when(f == 0)
    def _():
        acc_ref[...] = part

    @pl.when(f > 0)
    def _():
        acc_ref[...] += part

    @pl.when(f == pl.num_programs(1) - 1)
    def _():
        y_ref[...] = acc_ref[...].astype(y_ref.dtype)


def _grouped_experts(x_sorted, block_expert, block_xrow, n_used, n_blocks, w_gate, w_up, w_down):
    d = x_sorted.shape[1]
    dff = w_gate.shape[-1]
    tf = _tile(dff, EXPERT_FF_TILE)
    nf = dff // tf
    grid_spec = pltpu.PrefetchScalarGridSpec(
        num_scalar_prefetch=2, grid=(n_used, nf),
        in_specs=[
            pl.BlockSpec((pl.Element(EXPERT_ROWS), pl.Element(d)),
                         lambda b, f, be, bx: (pl.multiple_of(bx[b], X_ROW_ALIGN), 0)),
            pl.BlockSpec((None, d, tf), lambda b, f, be, bx: (be[b], 0, f)),
            pl.BlockSpec((None, d, tf), lambda b, f, be, bx: (be[b], 0, f)),
            pl.BlockSpec((None, tf, d), lambda b, f, be, bx: (be[b], f, 0)),
        ],
        out_specs=pl.BlockSpec((EXPERT_ROWS, d), lambda b, f, be, bx: (b, 0)),
        scratch_shapes=[pltpu.VMEM((EXPERT_ROWS, d), F32)],
    )
    return pl.pallas_call(
        _expert_kernel, grid_spec=grid_spec,
        out_shape=jax.ShapeDtypeStruct((n_blocks * EXPERT_ROWS, d), MXU_DTYPE),
        compiler_params=_cparams(("arbitrary", "arbitrary")), name="grouped_experts",
    )(block_expert, block_xrow, x_sorted, w_gate, w_up, w_down)


def _routed_experts(hb, idx, counts, w_gate, w_up, w_down):
    t, d = hb.shape
    e = w_gate.shape[0]
    a = t * TOP_K
    counts = counts.astype(jnp.int32)
    blocks_per = (counts + EXPERT_ROWS - 1) // EXPERT_ROWS
    blk_end = jnp.cumsum(blocks_per)
    blk_first = blk_end - blocks_per
    y_start = blk_first * EXPERT_ROWS
    x_rows = (counts + X_ROW_ALIGN - 1) // X_ROW_ALIGN * X_ROW_ALIGN
    x_start = jnp.cumsum(x_rows) - x_rows
    n_blocks = -(-a // EXPERT_ROWS) + e
    blocks = jnp.arange(n_blocks, dtype=jnp.int32)
    block_expert = jnp.minimum(jnp.searchsorted(blk_end, blocks, side="right"), e - 1).astype(jnp.int32)
    block_xrow = (x_start[block_expert] + (blocks - blk_first[block_expert]) * EXPERT_ROWS).astype(jnp.int32)
    n_used = blk_end[-1].astype(jnp.int32)
    dest_x, dest_y = _dest_rows(idx, jnp.stack([x_start, y_start]))
    px = a + X_ROW_ALIGN * e + EXPERT_ROWS
    slot_tok = (jnp.arange(px, dtype=jnp.int32) % t).at[dest_x[:, :TOP_K].reshape(a)].set(
        jnp.arange(a, dtype=jnp.int32) // TOP_K)
    x_sorted = hb[slot_tok]
    y = _grouped_experts(x_sorted, block_expert, block_xrow, n_used, n_blocks, w_gate, w_up, w_down)
    return y[dest_y[:, :TOP_K].T.reshape(a)]


def kernel(x_prompt, x_sample, state_ret, state_mlstm_c, state_mlstm_n, state_mlstm_m, state_conv, p_prompt, p_sample, w_in, b_gate_i, b_gate_f, conv_w, conv_b, g_ret, g_mlstm, w_br_r, w_br_m, w_out, ln1_g, ln1_b, w_router, b_router, w_e_gate, w_e_up, w_e_down, w_s_gate, w_s_up, w_s_down, ln2_g, ln2_b, w_pe, w_pg):
    depth = w_in.shape[0]
    assert depth == 1, "single-layer step"
    alpha = (2 * depth) ** 0.25
    bp, lp, d = x_prompt.shape
    bs, ls, _ = x_sample.shape
    assert bp == 1
    hd = state_ret.shape[-1]
    w = N_HEADS * hd
    conv_hist = state_conv.shape[2]
    off = {"rq": 0, "rk": w, "rv": 2 * w, "rg": 3 * w, "mq": 4 * w, "mk": 5 * w, "mv": 6 * w, "mo": 7 * w}
    off_mi = 8 * w
    n_main = 8 * w
    tp, ts = bp * lp, bs * ls
    t = tp + ts

    x = jnp.concatenate([x_prompt.reshape(tp, d), x_sample.reshape(ts, d)], axis=0)
    xb = _mx(x)
    w_in_t = jnp.swapaxes(w_in[0], 0, 1)
    first = lambda accs, extras: accs[0]

    z_main = _fused_matmul([xb], [(w_in_t, 0, 0)], [], first, [F32], n_main, rhs_t=True, name="proj_main")
    tn_tail = _tile(d, 512)
    n_tail = -(-(2 * d + 2 * N_HEADS) // tn_tail) * tn_tail
    z_tail = _fused_matmul([xb], [(w_in_t, off_mi, 0)], [], first, [F32], n_tail, tn=tn_tail, rhs_t=True,
                           name="proj_tail")
    z_if = z_tail[:, :2 * N_HEADS]

    def conv_tail(hist, row0, n_seq, length):
        k = min(conv_hist, length)
        ends = [row0 + (s + 1) * length for s in range(n_seq)]
        tail = jnp.stack([z_main[e - k:e, off["mq"]:off["mv"]] for e in ends])
        return jnp.concatenate([hist, tail], axis=1)[:, -conv_hist:][None]

    conv0_p = jnp.zeros((bp, conv_hist, 2 * w), F32)
    conv_p = conv_tail(conv0_p, 0, bp, lp)
    conv_s = conv_tail(state_conv[0], tp, bs, ls)

    pos = jnp.concatenate([jnp.tile(jnp.arange(lp, dtype=jnp.int32), bp),
                           jnp.tile(PAST_LEN + jnp.arange(ls, dtype=jnp.int32), bs)])
    half = hd // 2
    inv_freq = ROPE_BASE ** (-jnp.arange(half, dtype=F32) / half)
    ang = pos.astype(F32)[:, None] * inv_freq[None, :]
    cos, sin = jnp.cos(ang), jnp.sin(ang)

    mi = z_if[:, :N_HEADS] + b_gate_i[0]
    lf = jax.nn.log_sigmoid(z_if[:, N_HEADS:2 * N_HEADS] + b_gate_f[0])
    cl_p = _tile(lp, PROMPT_CHUNK)
    cl_s = ls

    def gate_layouts(rows, cl):
        b = jnp.cumsum(lf[rows].reshape(-1, cl, N_HEADS), axis=1)
        i = mi[rows].reshape(-1, cl, N_HEADS)
        col = jnp.concatenate([b, i], axis=-1).reshape(-1, 2 * N_HEADS)
        row = jnp.stack([jnp.swapaxes(b, 1, 2), jnp.swapaxes(i, 1, 2)], axis=2)
        return jnp.pad(col, ((0, 0), (0, 128 - 2 * N_HEADS))), jnp.pad(row, ((0, 0), (0, 0), (0, 6), (0, 0)))

    gcol_p, grow_p = gate_layouts(slice(0, tp), cl_p)
    gcol_s, grow_s = gate_layouts(slice(tp, t), cl_s)
    gcol = jnp.concatenate([gcol_p, gcol_s], axis=0)

    zeros = lambda *shape: jnp.zeros(shape, F32)
    rec = functools.partial(_recurrences, z_main, cos, sin, gcol, conv_w[0], conv_b, g_ret=g_ret, g_mlstm=g_mlstm,
                            offs=off)
    yr_p, ym_p, s_p, c_p, n_p, m_p = rec(
        grow_p, conv0_p, s0=zeros(bp, N_HEADS, hd, hd), c0=zeros(bp, N_HEADS, hd, hd),
        n0=zeros(bp, N_HEADS, 1, hd), m0=zeros(bp, N_HEADS, 1, 128), row0=0, n_seq=bp, seq_len=lp, cl=cl_p,
        name="recurrence_prompt")
    m0_s = jnp.broadcast_to(state_mlstm_m[0][:, :, None, None], (bs, N_HEADS, 1, 128))
    yr_s, ym_s, s_s, c_s, n_s, m_s = rec(
        grow_s, state_conv[0], s0=state_ret[0], c0=state_mlstm_c[0], n0=state_mlstm_n[0][:, :, None, :], m0=m0_s,
        row0=tp, n_seq=bs, seq_len=ls, cl=cl_s, name="recurrence_sample")
    ret_out = jnp.concatenate([yr_p, yr_s], axis=0)
    mls_out = jnp.concatenate([ym_p, ym_s], axis=0)

    def merge(accs, extras):
        sh = 2 * N_HEADS
        ga = jnp.concatenate([extras[0][:, sh:], extras[1][:, :sh]], axis=1)
        gb = jnp.concatenate([extras[2][:, sh:], extras[3][:, :sh]], axis=1)
        return jax.nn.sigmoid(ga) * accs[0] + jax.nn.sigmoid(gb) * accs[1]

    merged = _fused_matmul([ret_out, mls_out], [(w_br_r[0], 0, 0), (w_br_m[0], 0, 1)],
                           [(z_tail, 0), (z_tail, tn_tail), (z_tail, d), (z_tail, d + tn_tail)],
                           merge, [MXU_DTYPE], d, tm=512, tn=tn_tail, name="merge")

    u1 = _fused_matmul([merged], [(w_out[0], 0, 0)], [(x, 0)], lambda accs, extras: alpha * extras[0] + accs[0],
                       [F32], d, name="out_proj")
    h, hb, idx, gates, cnt = _ln_router(u1, ln1_g, ln1_b, w_router[0], b_router)

    yk = _routed_experts(hb, idx, cnt[:, 0, :].sum(0), w_e_gate[0], w_e_up[0], w_e_down[0])
    act = _fused_matmul([hb], [(w_s_gate[0], 0, 0), (w_s_up[0], 0, 0)], [],
                        lambda accs, extras: jax.nn.silu(accs[0]) * accs[1], [MXU_DTYPE], w_s_gate.shape[-1],
                        tm=512, tn=256, name="shared_up")

    def ffn_residual(accs, extras):
        routed = extras[2].astype(F32) * extras[1][:, 0:1]
        for k in range(1, TOP_K):
            routed = routed + extras[2 + k].astype(F32) * extras[1][:, k:k + 1]
        return alpha * extras[0] + (routed + accs[0])

    u2 = _fused_matmul([act], [(w_s_down[0], 0, 0)], [(h, 0), (gates, None)] + [(yk, 0, k * t) for k in range(TOP_K)],
                       ffn_residual, [F32], d, tm=512, name="shared_down")

    x2, x2b = _ln(u2, ln2_g, ln2_b)
    p = _mx(jnp.concatenate([p_prompt[0].reshape(tp, -1), p_sample[0].reshape(ts, -1)], axis=0))
    out = _fused_matmul([x2b, p], [(w_pg[0], 0, 0), (w_pe[0], 0, 1)], [(x2, 0)],
                        lambda accs, extras: extras[0] + jax.nn.sigmoid(accs[0]) * accs[1], [F32], d, tm=512,
                        name="embed_gate")

    dt = x_prompt.dtype
    y_p = out[:tp].reshape(bp, lp, d).astype(dt)
    y_s = out[tp:].reshape(bs, ls, d).astype(dt)
    return (y_p, y_s,
            s_p[None].astype(dt), c_p[None].astype(dt), n_p[:, :, 0][None].astype(dt), m_p[:, :, 0, 0][None].astype(dt),
            conv_p.astype(dt),
            s_s[None].astype(dt), c_s[None].astype(dt), n_s[:, :, 0][None].astype(dt), m_s[:, :, 0, 0][None].astype(dt),
            conv_s.astype(dt))
```
